```python
import math, functools
import jax, jax.numpy as jnp
from jax import lax
import numpy as np

D_MODEL = 4096
BATCH = 4
SEQ = 2048
DEPTH = 2
DEC_BATCH = 128
DEC_SEQ = 4
PAST_LEN = 16384
PAGE_SIZE = 128

MLA_HEADS = 32
Q_LORA = 896
KV_LORA = 512
QK_NOPE = 128
QK_ROPE = 64
V_HEAD = 128
ROPE_BASE = 10000.0
MLA_SCALE = (QK_NOPE + QK_ROPE) ** -0.5
Q_BLOCK = 128
DN_HEADS = 32
DN_DK = 128
DN_DV = 128
DN_CONV = 4
DN_CHUNK = 64
DN_QK = DN_HEADS * DN_DK
DN_V = DN_HEADS * DN_DV
DN_CONV_CH = 2 * DN_QK + DN_V
MEM_TOKENS = 256
MEM_HEADS = 4
MEM_HD = 128
D_FF = 11008
FFN_CONV = 3
N_BRANCH = 3
EPS = 1e-6
SPLITS = (Q_LORA, KV_LORA, QK_ROPE, DN_CONV_CH, DN_HEADS, DN_HEADS, DN_V, MEM_HEADS * MEM_HD, N_BRANCH * D_MODEL)
C_IN = sum(SPLITS)
SPLIT_POINTS = tuple(int(v) for v in np.cumsum(SPLITS)[:-1])

kernel_name = 'hybrid_mla_gdn_memory_convffn_step'


def rms_norm(x, g):
    xf = x.astype(jnp.float32)
    y = xf * lax.rsqrt(jnp.mean(xf * xf, axis=-1, keepdims=True) + EPS)
    return (y * g.astype(jnp.float32)).astype(x.dtype)


def l2_norm(x):
    xf = x.astype(jnp.float32)
    return (xf * lax.rsqrt(jnp.sum(xf * xf, axis=-1, keepdims=True) + EPS)).astype(x.dtype)


def rope(x, pos):
    half = QK_ROPE // 2
    inv_freq = ROPE_BASE ** (-jnp.arange(half, dtype=jnp.float32) / half)
    ang = pos.astype(jnp.float32)[:, None] * inv_freq[None, :]
    shape = (ang.shape[0],) + (1,) * (x.ndim - 3) + (half,)
    cos = jnp.cos(ang).reshape(shape)
    sin = jnp.sin(ang).reshape(shape)
    xf = x.astype(jnp.float32)
    x1, x2 = xf[..., :half], xf[..., half:]
    return jnp.concatenate([x1 * cos - x2 * sin, x2 * cos + x1 * sin], axis=-1).astype(x.dtype)


def causal_dwconv(x, hist, w):
    width = w.shape[0]
    t = x.shape[1]
    xp = jnp.concatenate([hist.astype(x.dtype), x], axis=1)
    y = xp[:, 0:t] * w[0]
    for i in range(1, width):
        y = y + xp[:, i:i + t] * w[i]
    return y, xp[:, t:]


def gated_delta_chunked(q, k, v, g, beta, s0):
    f32 = jnp.float32
    b, t, h, _ = q.shape
    dv = v.shape[-1]
    c = min(DN_CHUNK, t)
    n = -(-t // c)
    pad = n * c - t

    def blocks(a):
        a = jnp.pad(a.astype(f32), [(0, 0), (0, pad)] + [(0, 0)] * (a.ndim - 2))
        a = a.reshape((b, n, c) + a.shape[2:])
        return jnp.swapaxes(jnp.moveaxis(a, 3, 2), 0, 1)

    qc, kc, vc, gch, bc = blocks(q), blocks(k), blocks(v), blocks(g), blocks(beta)
    gcum = jnp.cumsum(gch, axis=-1)
    incl = jnp.tril(jnp.ones((c, c), bool))
    strict = jnp.tril(jnp.ones((c, c), bool), -1)
    decay = jnp.exp(jnp.where(incl, gcum[..., :, None] - gcum[..., None, :], -jnp.inf))
    kk = jnp.einsum('nbhid,nbhjd->nbhij', kc, kc)
    a_mat = jnp.where(strict, bc[..., :, None] * kk * decay, 0.0) + jnp.eye(c, dtype=f32)
    solve = functools.partial(lax.linalg.triangular_solve, left_side=True, lower=True, unit_diagonal=True)
    u_v = solve(a_mat, bc[..., None] * vc)
    w_k = solve(a_mat, (bc * jnp.exp(gcum))[..., None] * kc)
    qk = jnp.einsum('nbhid,nbhjd->nbhij', qc, kc) * decay
    q_dec = qc * jnp.exp(gcum)[..., None]
    k_dec = kc * jnp.exp(gcum[..., -1:] - gcum)[..., None]
    g_end = jnp.exp(gcum[..., -1])

    def step(s, xs_):
        u_v_c, w_c, qk_c, q_c, k_c, ge = xs_
        u = u_v_c - jnp.einsum('bhck,bhkv->bhcv', w_c, s)
        o = jnp.einsum('bhck,bhkv->bhcv', q_c, s) + jnp.einsum('bhij,bhjv->bhiv', qk_c, u)
        s = ge[..., None, None] * s + jnp.einsum('bhck,bhcv->bhkv', k_c, u)
        return s, o

    s_fin, o = lax.scan(step, s0.astype(f32), (u_v, w_k, qk, q_dec, k_dec, g_end))
    o = jnp.moveaxis(jnp.swapaxes(o, 0, 1), 2, 3).reshape(b, n * c, h, dv)[:, :t]
    return o.astype(v.dtype), s_fin.astype(s0.dtype)


def deltanet_branch(qkv, b_raw, a_raw, z, hist, s0, p):
    f32 = jnp.float32
    b, t, _ = qkv.shape
    y, hist_new = causal_dwconv(qkv, hist, p['dn_conv_w'])
    y = jax.nn.silu(y)
    q = l2_norm(y[..., :DN_QK].reshape(b, t, DN_HEADS, DN_DK)) * (DN_DK ** -0.5)
    k = l2_norm(y[..., DN_QK:2 * DN_QK].reshape(b, t, DN_HEADS, DN_DK))
    v = y[..., 2 * DN_QK:].reshape(b, t, DN_HEADS, DN_DV)
    beta = jax.nn.sigmoid(b_raw.astype(f32))
    g = -jnp.exp(p['dn_a_log'].astype(f32)) * jax.nn.softplus(a_raw.astype(f32) + p['dn_dt_bias'].astype(f32))
    o, s_new = gated_delta_chunked(q, k, v, g, beta, s0)
    o = rms_norm(o, p['dn_out_norm']) * jax.nn.silu(z.reshape(b, t, DN_HEADS, DN_DV))
    return o.reshape(b, t, DN_V), hist_new, s_new


def mla_project(cq, ckv_raw, kpe_raw, pos, p):
    b, t, _ = cq.shape
    q = rms_norm(cq, p['q_norm']) @ p['w_uq']
    q = rms_norm(q.reshape(b, t, MLA_HEADS, QK_NOPE + QK_ROPE), p['qh_norm'])
    q_lat = jnp.einsum('bthd,hdc->bthc', q[..., :QK_NOPE], p['w_uk'])
    q_abs = jnp.concatenate([q_lat, rope(q[..., QK_NOPE:], pos)], axis=-1)
    ckv = rms_norm(ckv_raw, p['kv_norm'])
    kpe = rope(rms_norm(kpe_raw, p['kpe_norm']), pos)
    return q_abs, ckv, kpe


def mla_attend_prompt(q_abs, ckv, kpe, w_uv):
    f32 = jnp.float32
    b, t, h, _ = q_abs.shape
    keys = jnp.concatenate([ckv, kpe], axis=-1)
    nb = t // Q_BLOCK
    qb = jnp.swapaxes(q_abs.reshape(b, nb, Q_BLOCK, h, -1), 0, 1)
    kpos = jnp.arange(t)

    def block(args):
        i, qi = args
        s = jnp.einsum('bqhc,bkc->bhqk', qi, keys, preferred_element_type=f32) * MLA_SCALE
        qpos = i * Q_BLOCK + jnp.arange(Q_BLOCK)
        s = jnp.where(kpos[None, :] <= qpos[:, None], s, -jnp.inf)
        pr = jax.nn.softmax(s, axis=-1).astype(ckv.dtype)
        return jnp.einsum('bhqk,bkc->bqhc', pr, ckv)

    o_lat = lax.map(block, (jnp.arange(nb), qb))
    o_lat = jnp.swapaxes(o_lat, 0, 1).reshape(b, t, h, KV_LORA)
    o = jnp.einsum('bthc,hcd->bthd', o_lat, w_uv)
    return o.reshape(b, t, h * V_HEAD)


def mla_attend_sample(q_abs, ckv_new, kpe_new, w_uv, cache_ckv, cache_kpe, page_table, layer):
    f32 = jnp.float32
    b, t, h, _ = q_abs.shape
    k_new = jnp.concatenate([ckv_new, kpe_new], axis=-1)
    s = jnp.einsum('bqhc,bkc->bhqk', q_abs, k_new, preferred_element_type=f32) * MLA_SCALE
    s = jnp.where(jnp.tril(jnp.ones((t, t), bool)), s, -jnp.inf)
    m0 = jnp.max(s, axis=-1)
    pr = jnp.exp(s - m0[..., None])
    den0 = jnp.sum(pr, axis=-1)
    acc0 = jnp.einsum('bhqk,bkc->bhqc', pr, ckv_new.astype(f32))

    def page_step(carry, phys):
        m, den, acc = carry
        ck = cache_ckv[layer, phys]
        keys = jnp.concatenate([ck, cache_kpe[layer, phys]], axis=-1).astype(q_abs.dtype)
        sp = jnp.einsum('bqhc,bkc->bhqk', q_abs, keys, preferred_element_type=f32) * MLA_SCALE
        m_new = jnp.maximum(m, jnp.max(sp, axis=-1))
        corr = jnp.exp(m - m_new)
        pp = jnp.exp(sp - m_new[..., None])
        den = den * corr + jnp.sum(pp, axis=-1)
        acc = acc * corr[..., None] + jnp.einsum('bhqk,bkc->bhqc', pp, ck.astype(f32))
        return (m_new, den, acc), None

    (m, den, acc), _ = lax.scan(page_step, (m0, den0, acc0), page_table.T)
    o_lat = (acc / den[..., None]).astype(q_abs.dtype)
    o = jnp.einsum('bhqc,hcd->bqhd', o_lat, w_uv)
    return o.reshape(b, t, h * V_HEAD)


def mem_kv(mem, g, w_kv, mk_g):
    b, m, _ = mem.shape
    kv = rms_norm(mem, g) @ w_kv
    k = rms_norm(kv[..., :MEM_HEADS * MEM_HD].reshape(b, m, MEM_HEADS, MEM_HD), mk_g)
    v = kv[..., MEM_HEADS * MEM_HD:].reshape(b, m, MEM_HEADS, MEM_HD)
    return k, v


def mem_attend(q, k, v):
    b, t = q.shape[:2]
    s = jnp.einsum('bthd,bmhd->bhtm', q, k, preferred_element_type=jnp.float32) * (MEM_HD ** -0.5)
    pr = jax.nn.softmax(s, axis=-1).astype(v.dtype)
    return jnp.einsum('bhtm,bmhd->bthd', pr, v).reshape(b, t, MEM_HEADS * MEM_HD)


def trunk_layer(x, pos, attend, mem_k, mem_v, dn_hist, dn_s0, ffn_hist, p):
    b, t, _ = x.shape
    h = rms_norm(x, p['norm_mix'])
    proj = h @ p['w_in']
    cq, ckv_raw, kpe_raw, dn_qkv, dn_b, dn_a, dn_z, mem_q, gate_raw = jnp.split(proj, SPLIT_POINTS, axis=-1)
    q_abs, ckv, kpe = mla_project(cq, ckv_raw, kpe_raw, pos, p)
    o_mla = attend(q_abs, ckv, kpe, p['w_uv'])
    o_dn, dn_hist_new, dn_s_new = deltanet_branch(dn_qkv, dn_b, dn_a, dn_z, dn_hist, dn_s0, p)
    mq = rms_norm(mem_q.reshape(b, t, MEM_HEADS, MEM_HD), p['mq_norm'])
    o_mem = mem_attend(mq, mem_k, mem_v)
    gates = jax.nn.sigmoid(gate_raw.reshape(b, t, N_BRANCH, D_MODEL))
    merged = (gates[:, :, 0] * (o_mla @ p['w_br_mla'])
              + gates[:, :, 1] * (o_dn @ p['w_br_dn'])
              + gates[:, :, 2] * (o_mem @ p['w_br_mem']))
    x = x + merged @ p['w_out']
    h = rms_norm(x, p['norm_ffn'])
    u, ffn_hist_new = causal_dwconv(h @ p['w_up'], ffn_hist, p['ffn_conv_w'])
    gate, up = jnp.split(u, 2, axis=-1)
    x = x + (jax.nn.silu(gate) * up) @ p['w_down']
    return x, ckv, kpe, dn_s_new, dn_hist_new, ffn_hist_new


def setup_inputs(seed: int = 0) -> dict:
    key = jax.random.key(seed)
    ks = iter(jax.random.split(key, 64))
    f32 = jnp.float32

    def nrm(shape, scale):
        return jax.random.normal(next(ks), shape, f32) * scale

    def gain(shape):
        return 1.0 + 0.02 * jax.random.normal(next(ks), shape, f32)

    n_pages = PAST_LEN // PAGE_SIZE
    n_pool = (DEC_BATCH * n_pages * 5) // 4
    page_table = jax.random.permutation(next(ks), n_pool)[:DEC_BATCH * n_pages].reshape(DEC_BATCH, n_pages).astype(jnp.int32)
    a_log = jnp.log(jax.random.uniform(next(ks), (DEPTH, DN_HEADS), f32, 1.0, 16.0))
    dt = jnp.exp(jax.random.uniform(next(ks), (DEPTH, DN_HEADS), f32, math.log(1e-3), math.log(0.1)))
    dt_bias = dt + jnp.log(-jnp.expm1(-dt))
    hq = QK_NOPE + QK_ROPE
    return {
        'x_prompt': nrm((BATCH, SEQ, D_MODEL), 1.0),
        'x_sample': nrm((DEC_BATCH, DEC_SEQ, D_MODEL), 1.0),
        'mem_prompt': nrm((BATCH, MEM_TOKENS, D_MODEL), 1.0),
        'cache_ckv': nrm((DEPTH, n_pool, PAGE_SIZE, KV_LORA), 1.0),
        'cache_kpe': nrm((DEPTH, n_pool, PAGE_SIZE, QK_ROPE), 1.0),
        'page_table': page_table,
        'cache_mem_k': nrm((DEPTH, DEC_BATCH, MEM_TOKENS, MEM_HEADS, MEM_HD), 1.0),
        'cache_mem_v': nrm((DEPTH, DEC_BATCH, MEM_TOKENS, MEM_HEADS, MEM_HD), 1.0),
        'state_dn': nrm((DEPTH, DEC_BATCH, DN_HEADS, DN_DK, DN_DV), 0.1),
        'state_dn_conv': nrm((DEPTH, DEC_BATCH, DN_CONV - 1, DN_CONV_CH), 1.0),
        'state_ffn_conv': nrm((DEPTH, DEC_BATCH, FFN_CONV - 1, 2 * D_FF), 1.0),
        'norm_mix': gain((DEPTH, D_MODEL)),
        'w_in': nrm((DEPTH, D_MODEL, C_IN), D_MODEL ** -0.5),
        'q_norm': gain((DEPTH, Q_LORA)),
        'kv_norm': gain((DEPTH, KV_LORA)),
        'kpe_norm': gain((DEPTH, QK_ROPE)),
        'qh_norm': gain((DEPTH, hq)),
        'w_uq': nrm((DEPTH, Q_LORA, MLA_HEADS * hq), Q_LORA ** -0.5),
        'w_uk': nrm((DEPTH, MLA_HEADS, QK_NOPE, KV_LORA), KV_LORA ** -0.5),
        'w_uv': nrm((DEPTH, MLA_HEADS, KV_LORA, V_HEAD), KV_LORA ** -0.5),
        'dn_conv_w': nrm((DEPTH, DN_CONV, DN_CONV_CH), DN_CONV ** -0.5),
        'dn_a_log': a_log,
        'dn_dt_bias': dt_bias,
        'dn_out_norm': gain((DEPTH, DN_DV)),
        'mem_norm': gain((DEPTH, D_MODEL)),
        'w_mem_kv': nrm((DEPTH, D_MODEL, 2 * MEM_HEADS * MEM_HD), D_MODEL ** -0.5),
        'mk_norm': gain((DEPTH, MEM_HD)),
        'mq_norm': gain((DEPTH, MEM_HD)),
        'w_br_mla': nrm((DEPTH, MLA_HEADS * V_HEAD, D_MODEL), (MLA_HEADS * V_HEAD) ** -0.5),
        'w_br_dn': nrm((DEPTH, DN_V, D_MODEL), DN_V ** -0.5),
        'w_br_mem': nrm((DEPTH, MEM_HEADS * MEM_HD, D_MODEL), (MEM_HEADS * MEM_HD) ** -0.5),
        'w_out': nrm((DEPTH, D_MODEL, D_MODEL), D_MODEL ** -0.5),
        'norm_ffn': gain((DEPTH, D_MODEL)),
        'w_up': nrm((DEPTH, D_MODEL, 2 * D_FF), D_MODEL ** -0.5),
        'ffn_conv_w': nrm((DEPTH, FFN_CONV, 2 * D_FF), FFN_CONV ** -0.5),
        'w_down': nrm((DEPTH, D_FF, D_MODEL), D_FF ** -0.5),
    }


def reference(x_prompt, x_sample, mem_prompt, cache_ckv, cache_kpe, page_table,
              cache_mem_k, cache_mem_v, state_dn, state_dn_conv, state_ffn_conv,
              norm_mix, w_in, q_norm, kv_norm, kpe_norm, qh_norm, w_uq, w_uk, w_uv,
              dn_conv_w, dn_a_log, dn_dt_bias, dn_out_norm,
              mem_norm, w_mem_kv, mk_norm, mq_norm,
              w_br_mla, w_br_dn, w_br_mem, w_out,
              norm_ffn, w_up, ffn_conv_w, w_down):
    bp, tp = x_prompt.shape[:2]
    ts = x_sample.shape[1]
    past_len = page_table.shape[1] * cache_ckv.shape[2]
    pos_p = jnp.arange(tp, dtype=jnp.int32)
    pos_s = past_len + jnp.arange(ts, dtype=jnp.int32)
    xp, xs = x_prompt, x_sample
    states_p, states_s = [], []
    for l in range(DEPTH):
        p = {'norm_mix': norm_mix[l], 'w_in': w_in[l], 'q_norm': q_norm[l], 'kv_norm': kv_norm[l],
             'kpe_norm': kpe_norm[l], 'qh_norm': qh_norm[l], 'w_uq': w_uq[l], 'w_uk': w_uk[l], 'w_uv': w_uv[l],
             'dn_conv_w': dn_conv_w[l], 'dn_a_log': dn_a_log[l], 'dn_dt_bias': dn_dt_bias[l],
             'dn_out_norm': dn_out_norm[l], 'mq_norm': mq_norm[l],
             'w_br_mla': w_br_mla[l], 'w_br_dn': w_br_dn[l], 'w_br_mem': w_br_mem[l], 'w_out': w_out[l],
             'norm_ffn': norm_ffn[l], 'w_up': w_up[l], 'ffn_conv_w': ffn_conv_w[l], 'w_down': w_down[l]}
        mk_p, mv_p = mem_kv(mem_prompt, mem_norm[l], w_mem_kv[l], mk_norm[l])
        xp, ckv_p, kpe_p, dns_p, dnh_p, ffh_p = trunk_layer(
            xp, pos_p, mla_attend_prompt, mk_p, mv_p,
            jnp.zeros((bp, DN_CONV - 1, DN_CONV_CH), xp.dtype),
            jnp.zeros((bp, DN_HEADS, DN_DK, DN_DV), state_dn.dtype),
            jnp.zeros((bp, FFN_CONV - 1, 2 * D_FF), xp.dtype), p)
        states_p.append((ckv_p, kpe_p, dns_p, dnh_p, ffh_p, mk_p, mv_p))
        attend_s = functools.partial(mla_attend_sample, cache_ckv=cache_ckv, cache_kpe=cache_kpe,
                                     page_table=page_table, layer=l)
        xs, ckv_s, kpe_s, dns_s, dnh_s, ffh_s = trunk_layer(
            xs, pos_s, attend_s, cache_mem_k[l], cache_mem_v[l],
            state_dn_conv[l], state_dn[l], state_ffn_conv[l], p)
        states_s.append((ckv_s, kpe_s, dns_s, dnh_s, ffh_s))
    (ckv_prompt, kpe_prompt, dn_state_prompt, dn_conv_prompt, ffn_conv_prompt,
     mem_k_prompt, mem_v_prompt) = [jnp.stack(a) for a in zip(*states_p)]
    (ckv_sample, kpe_sample, dn_state_sample, dn_conv_sample,
     ffn_conv_sample) = [jnp.stack(a) for a in zip(*states_s)]
    return (xp, xs, ckv_prompt, kpe_prompt, dn_state_prompt, dn_conv_prompt, ffn_conv_prompt,
            mem_k_prompt, mem_v_prompt, ckv_sample, kpe_sample, dn_state_sample, dn_conv_sample,
            ffn_conv_sample)
```

```python
import functools
import math

import jax
import jax.numpy as jnp
from jax import lax
from jax.experimental import pallas as pl
from jax.experimental.pallas import tpu as pltpu

F32 = jnp.float32
BF16 = jnp.bfloat16
EPS = 1e-6
ROPE_BASE = 10000.0
LANES = 128
SUBLANES = 8
VMEM_LIMIT_BYTES = 56 * 1024 * 1024
HI = lax.Precision.HIGHEST

DN_CHUNK = 64
DN_HEAD = 128
MEM_HD = 128
QK_NOPE = 128
QK_ROPE = 64


def _cp(*sem):
    return pltpu.CompilerParams(dimension_semantics=sem, vmem_limit_bytes=VMEM_LIMIT_BYTES)


def _rms(x, g):
    return x * lax.rsqrt(jnp.mean(x * x, axis=-1, keepdims=True) + EPS) * g


def _silu(x):
    return x * (1.0 / (1.0 + jnp.exp(-x)))


def _sigmoid(x):
    return 1.0 / (1.0 + jnp.exp(-x))


def _softplus(x):
    return jnp.maximum(x, 0.0) + jnp.log(1.0 + jnp.exp(-jnp.abs(x)))


def _dot(a, b):
    return jnp.dot(a, b, preferred_element_type=F32)


def _dot_nt(a, b):
    return lax.dot_general(a, b, (((1,), (1,)), ((), ())), preferred_element_type=F32)


def _rmsnorm_body(x_ref, g_ref, o_ref):
    o_ref[...] = _rms(x_ref[...].astype(F32), g_ref[...]).astype(o_ref.dtype)


def rmsnorm(x, g, out_dtype, tm):
    n, d = x.shape
    return pl.pallas_call(
        _rmsnorm_body,
        grid=(n // tm,),
        in_specs=[pl.BlockSpec((tm, d), lambda i: (i, 0)), pl.BlockSpec((1, d), lambda i: (0, 0))],
        out_specs=pl.BlockSpec((tm, d), lambda i: (i, 0)),
        out_shape=jax.ShapeDtypeStruct((n, d), out_dtype),
        compiler_params=_cp("parallel"),
        name="rmsnorm",
    )(x, g.reshape(1, d).astype(F32))


def _headnorm_body(x_ref, g_ref, o_ref, *, hd):
    x = x_ref[...]
    for h in range(x.shape[1] // hd):
        o_ref[:, h * hd:(h + 1) * hd] = _rms(x[:, h * hd:(h + 1) * hd], g_ref[...])


def headnorm(x, g, hd, tm):
    n, d = x.shape
    return pl.pallas_call(
        functools.partial(_headnorm_body, hd=hd),
        grid=(n // tm,),
        in_specs=[pl.BlockSpec((tm, d), lambda i: (i, 0)), pl.BlockSpec((1, hd), lambda i: (0, 0))],
        out_specs=pl.BlockSpec((tm, d), lambda i: (i, 0)),
        out_shape=jax.ShapeDtypeStruct((n, d), F32),
        compiler_params=_cp("parallel"),
        name="headnorm",
    )(x, g.reshape(1, hd).astype(F32))


def _mm_body(*refs, nk, has_res):
    if has_res:
        a_ref, b_ref, r_ref, o_ref = refs[:4]
    else:
        a_ref, b_ref, o_ref = refs[:3]
        r_ref = None
    if nk == 1:
        acc = _dot(a_ref[...], b_ref[...])
        if has_res:
            acc = r_ref[...] + acc
        o_ref[...] = acc.astype(o_ref.dtype)
        return
    acc_ref = refs[-1]
    k = pl.program_id(2)

    @pl.when(k == 0)
    def _():
        acc_ref[...] = jnp.zeros_like(acc_ref)

    acc_ref[...] += _dot(a_ref[...], b_ref[...])

    @pl.when(k == nk - 1)
    def _():
        acc = acc_ref[...]
        if has_res:
            acc = r_ref[...] + acc
        o_ref[...] = acc.astype(o_ref.dtype)


def matmul(a, b, out_dtype, tm, tn, tk=None, res=None):
    m, kk = a.shape
    n = b.shape[1]
    tk = kk if tk is None else tk
    nk = kk // tk
    assert m % tm == 0 and n % tn == 0 and kk % tk == 0
    in_specs = [pl.BlockSpec((tm, tk), lambda i, j, k: (i, k)), pl.BlockSpec((tk, tn), lambda i, j, k: (k, j))]
    args = [a, b]
    if res is not None:
        in_specs.append(pl.BlockSpec((tm, tn), lambda i, j, k: (i, j)))
        args.append(res)
    return pl.pallas_call(
        functools.partial(_mm_body, nk=nk, has_res=res is not None),
        grid=(m // tm, n // tn, nk),
        in_specs=in_specs,
        out_specs=pl.BlockSpec((tm, tn), lambda i, j, k: (i, j)),
        out_shape=jax.ShapeDtypeStruct((m, n), out_dtype),
        scratch_shapes=[pltpu.VMEM((tm, tn), F32)] if nk > 1 else [],
        compiler_params=_cp("parallel", "parallel", "arbitrary"),
        name="matmul",
    )(*args)


def _rope(x, cos2, sin2):
    half = x.shape[1] // 2
    rot = jnp.concatenate([x[:, half:], x[:, :half]], axis=1)
    return x * cos2 + rot * sin2


def _mla_prep_body(small_ref, qn_ref, kvn_ref, kpn_ref, qhn_ref, wuq_ref, wuk_ref, cos_ref, sin_ref,
                   ql_ref, qr_ref, ckv_ref, ckvb_ref, kpe_ref, kpeb_ref, cqn_ref, *, q_lora, kv_lora, scale):
    h = pl.program_id(1)

    @pl.when(h == 0)
    def _():
        cqn_ref[...] = _rms(small_ref[:, 0:q_lora], qn_ref[...]).astype(BF16)
        ckv = _rms(small_ref[:, q_lora:q_lora + kv_lora], kvn_ref[...])
        ckv_ref[...] = ckv
        ckvb_ref[...] = ckv.astype(BF16)
        o = q_lora + kv_lora
        kpe = _rope(_rms(small_ref[:, o:o + QK_ROPE], kpn_ref[...]), cos_ref[...], sin_ref[...])
        kpe_ref[...] = kpe
        kpeb_ref[...] = kpe.astype(BF16)

    q = _rms(_dot(cqn_ref[...], wuq_ref[0]), qhn_ref[...])
    ql = _dot(q[:, :QK_NOPE].astype(BF16), wuk_ref[0]) * scale
    ql_ref[0] = ql.astype(BF16)
    qr_ref[0] = (_rope(q[:, QK_NOPE:], cos_ref[...], sin_ref[...]) * scale).astype(BF16)


def mla_prep(small, q_norm, kv_norm, kpe_norm, qh_norm, wuq_h, wuk, cos2, sin2, tm, scale):
    n = small.shape[0]
    nh, q_lora, hq = wuq_h.shape
    kv_lora = wuk.shape[2]
    nrep = cos2.shape[0] // tm
    row = lambda i, h: (i, 0)
    const = lambda i, h: (0, 0)
    tab = lambda i, h: (i % nrep, 0)
    return pl.pallas_call(
        functools.partial(_mla_prep_body, q_lora=q_lora, kv_lora=kv_lora, scale=scale),
        grid=(n // tm, nh),
        in_specs=[
            pl.BlockSpec((tm, small.shape[1]), row),
            pl.BlockSpec((1, q_lora), const), pl.BlockSpec((1, kv_lora), const),
            pl.BlockSpec((1, QK_ROPE), const), pl.BlockSpec((1, hq), const),
            pl.BlockSpec((1, q_lora, hq), lambda i, h: (h, 0, 0)),
            pl.BlockSpec((1, QK_NOPE, kv_lora), lambda i, h: (h, 0, 0)),
            pl.BlockSpec((tm, QK_ROPE), tab), pl.BlockSpec((tm, QK_ROPE), tab),
        ],
        out_specs=[
            pl.BlockSpec((1, tm, kv_lora), lambda i, h: (h, i, 0)),
            pl.BlockSpec((1, tm, QK_ROPE), lambda i, h: (h, i, 0)),
            pl.BlockSpec((tm, kv_lora), row), pl.BlockSpec((tm, kv_lora), row),
            pl.BlockSpec((tm, QK_ROPE), row), pl.BlockSpec((tm, QK_ROPE), row),
        ],
        out_shape=[
            jax.ShapeDtypeStruct((nh, n, kv_lora), BF16), jax.ShapeDtypeStruct((nh, n, QK_ROPE), BF16),
            jax.ShapeDtypeStruct((n, kv_lora), F32), jax.ShapeDtypeStruct((n, kv_lora), BF16),
            jax.ShapeDtypeStruct((n, QK_ROPE), F32), jax.ShapeDtypeStruct((n, QK_ROPE), BF16),
        ],
        scratch_shapes=[pltpu.VMEM((tm, q_lora), BF16)],
        compiler_params=_cp("parallel", "arbitrary"),
        name="mla_prep",
    )(small, q_norm.reshape(1, -1), kv_norm.reshape(1, -1), kpe_norm.reshape(1, -1), qh_norm.reshape(1, -1),
      wuq_h, wuk, cos2, sin2)


def _flash_body(ql_ref, qr_ref, ckv_ref, kpe_ref, o_ref, m_ref, l_ref, acc_ref, *, tq):
    qi = pl.program_id(1)
    ql = ql_ref[0]
    qr = qr_ref[0]
    m_ref[...] = jnp.full_like(m_ref, -jnp.inf)
    l_ref[...] = jnp.zeros_like(l_ref)
    acc_ref[...] = jnp.zeros_like(acc_ref)

    def block(j, masked):
        ks = pl.multiple_of(j * tq, tq)
        kc = ckv_ref[pl.ds(ks, tq), :]
        s = _dot_nt(ql, kc) + _dot_nt(qr, kpe_ref[pl.ds(ks, tq), :])
        if masked:
            r = lax.broadcasted_iota(jnp.int32, (tq, tq), 0)
            c = lax.broadcasted_iota(jnp.int32, (tq, tq), 1)
            s = jnp.where(c <= r, s, -jnp.inf)
        m_old = m_ref[...]
        m_new = jnp.maximum(m_old, jnp.max(s, axis=1, keepdims=True))
        p = jnp.exp(s - m_new)
        corr = jnp.exp(m_old - m_new)
        l_ref[...] = l_ref[...] * corr + jnp.sum(p, axis=1, keepdims=True)
        acc_ref[...] = acc_ref[...] * corr + _dot(p.astype(BF16), kc)
        m_ref[...] = m_new

    block(qi, True)

    def body(j, carry):
        block(j, False)
        return carry

    lax.fori_loop(0, qi, body, 0)
    o_ref[0] = (acc_ref[...] / l_ref[...]).astype(o_ref.dtype)


def flash_prompt(ql, qr, ckvb, kpeb, bsz, seq, tq):
    nh, n, kv_lora = ql.shape
    nq = seq // tq
    qmap = lambda b, qi, h: (h, b * nq + qi, 0)
    kmap = lambda b, qi, h: (b, 0)
    return pl.pallas_call(
        functools.partial(_flash_body, tq=tq),
        grid=(bsz, nq, nh),
        in_specs=[
            pl.BlockSpec((1, tq, kv_lora), qmap), pl.BlockSpec((1, tq, QK_ROPE), qmap),
            pl.BlockSpec((seq, kv_lora), kmap), pl.BlockSpec((seq, QK_ROPE), kmap),
        ],
        out_specs=pl.BlockSpec((1, tq, kv_lora), qmap),
        out_shape=jax.ShapeDtypeStruct((nh, n, kv_lora), BF16),
        scratch_shapes=[pltpu.VMEM((tq, 1), F32), pltpu.VMEM((tq, 1), F32), pltpu.VMEM((tq, kv_lora), F32)],
        compiler_params=_cp("parallel", "parallel", "arbitrary"),
        name="flash_prompt",
    )(ql, qr, ckvb, kpeb)


def _paged_body(pt_ref, ql_ref, qr_ref, cn_ref, pn_ref, *refs, pg, t_new):
    ck_refs = refs[:pg]
    kp_refs = refs[pg:2 * pg]
    o_ref, m_ref, l_ref, acc_ref = refs[2 * pg:]
    p = pl.program_id(1)
    ql = ql_ref[0]
    qr = qr_ref[0]
    rows = ql.shape[0]

    @pl.when(p == 0)
    def _():
        cn = cn_ref[0].astype(BF16)
        s = _dot_nt(ql, cn) + _dot_nt(qr, pn_ref[0].astype(BF16))
        r = lax.rem(lax.broadcasted_iota(jnp.int32, s.shape, 0), t_new)
        c = lax.broadcasted_iota(jnp.int32, s.shape, 1)
        s = jnp.where(c <= r, s, -jnp.inf)
        m = jnp.max(s, axis=1, keepdims=True)
        pr = jnp.exp(s - m)
        m_ref[...] = m
        l_ref[...] = jnp.sum(pr, axis=1, keepdims=True)
        acc_ref[...] = _dot(pr.astype(BF16), cn)

    kcs = [ck_refs[j][0, 0].astype(BF16) for j in range(pg)]
    ss = [_dot_nt(ql, kcs[j]) + _dot_nt(qr, kp_refs[j][0, 0].astype(BF16)) for j in range(pg)]
    m_old = m_ref[...]
    m_new = m_old
    for s in ss:
        m_new = jnp.maximum(m_new, jnp.max(s, axis=1, keepdims=True))
    corr = jnp.exp(m_old - m_new)
    l_new = l_ref[...] * corr
    acc = acc_ref[...] * corr
    for j in range(pg):
        pr = jnp.exp(ss[j] - m_new)
        l_new = l_new + jnp.sum(pr, axis=1, keepdims=True)
        acc = acc + _dot(pr.astype(BF16), kcs[j])
    m_ref[...] = m_new
    l_ref[...] = l_new
    acc_ref[...] = acc

    @pl.when(p == pl.num_programs(1) - 1)
    def _():
        o_ref[0] = (acc_ref[...] / l_ref[...]).astype(o_ref.dtype)


def paged_sample(ql, qr, ckv_new, kpe_new, cache_ckv, cache_kpe, page_table, layer, pg):
    bsz, rows, kv_lora = ql.shape
    t_new = ckv_new.shape[1]
    n_pages = page_table.shape[1]
    page = cache_ckv.shape[2]
    assert n_pages % pg == 0
    fixed = lambda b, p, pt: (b, 0, 0)

    def page_map(j):
        return lambda b, p, pt: (layer, pt[b, p * pg + j], 0, 0)

    in_specs = [
        pl.BlockSpec((1, rows, kv_lora), fixed), pl.BlockSpec((1, rows, QK_ROPE), fixed),
        pl.BlockSpec((1, t_new, kv_lora), fixed), pl.BlockSpec((1, t_new, QK_ROPE), fixed),
    ]
    in_specs += [pl.BlockSpec((1, 1, page, kv_lora), page_map(j)) for j in range(pg)]
    in_specs += [pl.BlockSpec((1, 1, page, QK_ROPE), page_map(j)) for j in range(pg)]
    return pl.pallas_call(
        functools.partial(_paged_body, pg=pg, t_new=t_new),
        grid_spec=pltpu.PrefetchScalarGridSpec(
            num_scalar_prefetch=1,
            grid=(bsz, n_pages // pg),
            in_specs=in_specs,
            out_specs=pl.BlockSpec((1, rows, kv_lora), fixed),
            scratch_shapes=[pltpu.VMEM((rows, 1), F32), pltpu.VMEM((rows, 1), F32), pltpu.VMEM((rows, kv_lora), F32)],
        ),
        out_shape=jax.ShapeDtypeStruct((bsz, rows, kv_lora), BF16),
        compiler_params=_cp("parallel", "arbitrary"),
        name="paged_sample",
    )(page_table, ql, qr, ckv_new, kpe_new, *([cache_ckv] * pg), *([cache_kpe] * pg))


def _uv_body(o_ref, w_ref, out_ref):
    out_ref[...] = _dot(o_ref[0], w_ref[0]).astype(out_ref.dtype)


def uv_project(o_lat, wuv, tm):
    nh, n, kv_lora = o_lat.shape
    vh = wuv.shape[2]
    return pl.pallas_call(
        _uv_body,
        grid=(n // tm, nh),
        in_specs=[pl.BlockSpec((1, tm, kv_lora), lambda i, h: (h, i, 0)),
                  pl.BlockSpec((1, kv_lora, vh), lambda i, h: (h, 0, 0))],
        out_specs=pl.BlockSpec((tm, vh), lambda i, h: (i, h)),
        out_shape=jax.ShapeDtypeStruct((n, nh * vh), BF16),
        compiler_params=_cp("parallel", "parallel"),
        name="uv_project",
    )(o_lat, wuv)


def _mem_attn_body(q_ref, k_ref, v_ref, g_ref, o_ref, *, nh):
    q = q_ref[0]
    for h in range(nh):
        sl = slice(h * MEM_HD, (h + 1) * MEM_HD)
        qh = _rms(q[:, sl], g_ref[...]).astype(BF16)
        s = _dot_nt(qh, k_ref[0, :, sl].astype(BF16)) * (MEM_HD ** -0.5)
        m = jnp.max(s, axis=1, keepdims=True)
        p = jnp.exp(s - m)
        l = jnp.sum(p, axis=1, keepdims=True)
        o = _dot(p.astype(BF16), v_ref[0, :, sl].astype(BF16)) / l
        o_ref[0, :, sl] = o.astype(o_ref.dtype)


def mem_attend(q, k, v, mq_norm, tq):
    bsz, t, d = q.shape
    m = k.shape[1]
    return pl.pallas_call(
        functools.partial(_mem_attn_body, nh=d // MEM_HD),
        grid=(bsz, t // tq),
        in_specs=[pl.BlockSpec((1, tq, d), lambda b, i: (b, i, 0)),
                  pl.BlockSpec((1, m, d), lambda b, i: (b, 0, 0)),
                  pl.BlockSpec((1, m, d), lambda b, i: (b, 0, 0)),
                  pl.BlockSpec((1, MEM_HD), lambda b, i: (0, 0))],
        out_specs=pl.BlockSpec((1, tq, d), lambda b, i: (b, i, 0)),
        out_shape=jax.ShapeDtypeStruct((bsz, t, d), BF16),
        compiler_params=_cp("parallel", "parallel"),
        name="mem_attend",
    )(q, k, v, mq_norm.reshape(1, MEM_HD))


def _conv_rows(x_ref, prev_ref, hist_ref, w_ref, xs_ref, is_start, width):
    tm = x_ref.shape[0]
    xs_ref[0:SUBLANES, :] = jnp.where(is_start, hist_ref[0], prev_ref[...])
    xs_ref[SUBLANES:, :] = x_ref[...]
    y = xs_ref[pl.ds(SUBLANES, tm), :] * w_ref[width - 1:width, :]
    for i in range(width - 1):
        y = y + xs_ref[pl.ds(SUBLANES - (width - 1) + i, tm), :] * w_ref[i:i + 1, :]
    return y


def _conv_silu_body(x_ref, prev_ref, hist_ref, w_ref, o_ref, xs_ref, *, width, tiles_per_seq):
    is_start = pl.program_id(1) % tiles_per_seq == 0
    o_ref[...] = _silu(_conv_rows(x_ref, prev_ref, hist_ref, w_ref, xs_ref, is_start, width))


def _conv_swiglu_body(g_ref, gp_ref, gh_ref, gw_ref, u_ref, up_ref, uh_ref, uw_ref, o_ref, gs_ref, us_ref, *,
                      width, tiles_per_seq):
    is_start = pl.program_id(1) % tiles_per_seq == 0
    gate = _conv_rows(g_ref, gp_ref, gh_ref, gw_ref, gs_ref, is_start, width)
    up = _conv_rows(u_ref, up_ref, uh_ref, uw_ref, us_ref, is_start, width)
    o_ref[...] = (_silu(gate) * up).astype(o_ref.dtype)


def _hist8(hist):
    return jnp.pad(hist, ((0, 0), (SUBLANES - hist.shape[1], 0), (0, 0)))


def conv_silu_rows(x, hist, w, seq, tm, tc):
    n, c = x.shape
    width = w.shape[0]
    tps = seq // tm
    r8 = tm // SUBLANES
    return pl.pallas_call(
        functools.partial(_conv_silu_body, width=width, tiles_per_seq=tps),
        grid=(c // tc, n // tm),
        in_specs=[pl.BlockSpec((tm, tc), lambda j, i: (i, j)),
                  pl.BlockSpec((SUBLANES, tc), lambda j, i: (jnp.maximum(i * r8 - 1, 0), j)),
                  pl.BlockSpec((1, SUBLANES, tc), lambda j, i: (i // tps, 0, j)),
                  pl.BlockSpec((width, tc), lambda j, i: (0, j))],
        out_specs=pl.BlockSpec((tm, tc), lambda j, i: (i, j)),
        out_shape=jax.ShapeDtypeStruct((n, c), F32),
        scratch_shapes=[pltpu.VMEM((tm + SUBLANES, tc), F32)],
        compiler_params=_cp("parallel", "parallel"),
        name="conv_silu_rows",
    )(x, x, _hist8(hist), w)


def conv_swiglu_rows(u, hist, w, seq, tm, tc):
    n, c2 = u.shape
    f = c2 // 2
    width = w.shape[0]
    tps = seq // tm
    r8 = tm // SUBLANES
    nf = f // tc
    h8 = _hist8(hist)

    def specs(off):
        return [pl.BlockSpec((tm, tc), lambda j, i: (i, j + off)),
                pl.BlockSpec((SUBLANES, tc), lambda j, i: (jnp.maximum(i * r8 - 1, 0), j + off)),
                pl.BlockSpec((1, SUBLANES, tc), lambda j, i: (i // tps, 0, j + off)),
                pl.BlockSpec((width, tc), lambda j, i: (0, j + off))]

    return pl.pallas_call(
        functools.partial(_conv_swiglu_body, width=width, tiles_per_seq=tps),
        grid=(nf, n // tm),
        in_specs=specs(0) + specs(nf),
        out_specs=pl.BlockSpec((tm, tc), lambda j, i: (i, j)),
        out_shape=jax.ShapeDtypeStruct((n, f), BF16),
        scratch_shapes=[pltpu.VMEM((tm + SUBLANES, tc), F32), pltpu.VMEM((tm + SUBLANES, tc), F32)],
        compiler_params=_cp("parallel", "parallel"),
        name="conv_swiglu_rows",
    )(u, u, h8, w, u, u, h8, w)


def _conv_tm(x_ref, h_ref, w_ref, width):
    t = x_ref.shape[0]
    rows = [h_ref[i] for i in range(width - 1)] + [x_ref[i] for i in range(t)]
    out = []
    for s in range(t):
        y = rows[s] * w_ref[0:1, :]
        for i in range(1, width):
            y = y + rows[s + i] * w_ref[i:i + 1, :]
        out.append(y)
    return out


def _conv_silu_tm_body(x_ref, h_ref, w_ref, o_ref, *, width):
    for s, y in enumerate(_conv_tm(x_ref, h_ref, w_ref, width)):
        o_ref[s] = _silu(y)


def _conv_swiglu_tm_body(g_ref, gh_ref, gw_ref, u_ref, uh_ref, uw_ref, o_ref, *, width):
    gate = _conv_tm(g_ref, gh_ref, gw_ref, width)
    up = _conv_tm(u_ref, uh_ref, uw_ref, width)
    for s in range(len(gate)):
        o_ref[s] = (_silu(gate[s]) * up[s]).astype(o_ref.dtype)


def conv_silu_tm(x, hist, w, tc):
    t, b, c = x.shape
    width = w.shape[0]
    return pl.pallas_call(
        functools.partial(_conv_silu_tm_body, width=width),
        grid=(c // tc,),
        in_specs=[pl.BlockSpec((t, b, tc), lambda j: (0, 0, j)),
                  pl.BlockSpec((width - 1, b, tc), lambda j: (0, 0, j)),
                  pl.BlockSpec((width, tc), lambda j: (0, j))],
        out_specs=pl.BlockSpec((t, b, tc), lambda j: (0, 0, j)),
        out_shape=jax.ShapeDtypeStruct((t, b, c), F32),
        compiler_params=_cp("parallel"),
        name="conv_silu_tm",
    )(x, hist, w)


def conv_swiglu_tm(u, hist, w, tc):
    t, b, c2 = u.shape
    f = c2 // 2
    nf = f // tc
    width = w.shape[0]

    def specs(off):
        return [pl.BlockSpec((t, b, tc), lambda j: (0, 0, j + off)),
                pl.BlockSpec((width - 1, b, tc), lambda j: (0, 0, j + off)),
                pl.BlockSpec((width, tc), lambda j: (0, j + off))]

    return pl.pallas_call(
        functools.partial(_conv_swiglu_tm_body, width=width),
        grid=(nf,),
        in_specs=specs(0) + specs(nf),
        out_specs=pl.BlockSpec((t, b, tc), lambda j: (0, 0, j)),
        out_shape=jax.ShapeDtypeStruct((t, b, f), BF16),
        compiler_params=_cp("parallel"),
        name="conv_swiglu_tm",
    )(u, hist, w, u, hist, w)


_NN = ((2,), (1,))
_NT = ((2,), (2,))
_TN = ((1,), (1,))


def _bdot(a, b, dims, exact=False):
    if exact:
        return lax.dot_general(a, b, (dims, ((0,), (0,))), precision=HI, preferred_element_type=F32)
    return lax.dot_general(a.astype(BF16), b.astype(BF16), (dims, ((0,), (0,))), preferred_element_type=F32)


def _l2n(x):
    return x * lax.rsqrt(jnp.sum(x * x, axis=-1, keepdims=True) + EPS)


def _dn_chunk_terms(q, k, v, beta, g):
    n, c, _ = q.shape
    ii = lax.broadcasted_iota(jnp.int32, (c, c), 0)
    jj = lax.broadcasted_iota(jnp.int32, (c, c), 1)
    incl = jj <= ii
    gm = jnp.broadcast_to(g, (n, c, c))
    gc = _bdot(jnp.broadcast_to(incl.astype(F32)[None], (n, c, c)), gm, _NN, True)
    gr = _bdot(jnp.ones((n, c, c), F32), jnp.where(ii <= jj, gm, 0.0), _NN, True)
    decay = jnp.where(incl, jnp.exp(gc - gr), 0.0)
    gcol = gc[:, :, 0:1]
    eg = jnp.exp(gcol)
    gend = gc[:, c - 1:c, 0:1]
    kk = _bdot(k, k, _NT, True)
    lo = jnp.where(jj < ii, beta * kk * decay, 0.0)
    inv = jnp.where(ii == jj, 1.0, 0.0) - lo
    pw = lo
    for _ in range(int(math.log2(c)) - 1):
        pw = _bdot(pw, pw, _NN)
        inv = inv + _bdot(inv, pw, _NN)
    u_v = _bdot(inv, beta * v, _NN)
    w_k = _bdot(inv, (beta * eg) * k, _NN)
    qk = _bdot(q, k, _NT) * decay
    return u_v, w_k, qk, q * eg, k * jnp.exp(gend - gcol), jnp.exp(gend)


def _dn_out(o, gn, z):
    return (_rms(o, gn) * _silu(z)).astype(BF16)


def _dn_prompt_body(alog_ref, dtb_ref, q_ref, k_ref, v_ref, ba_ref, z_ref, gn_ref, s0_ref, o_ref, s_ref,
                    uv_s, wk_s, qk_s, qd_s, kd_s, ge_s, *, chunk, b_lane, a_lane):
    h = pl.program_id(1)
    t = q_ref.shape[0]
    n = t // chunk
    q = _l2n(q_ref[...]) * (DN_HEAD ** -0.5)
    k = _l2n(k_ref[...])
    lane = lax.broadcasted_iota(jnp.int32, (1, LANES), 1)
    ba = ba_ref[...]
    b_raw = jnp.sum(jnp.where(lane == b_lane + h, ba, 0.0), axis=1, keepdims=True)
    a_raw = jnp.sum(jnp.where(lane == a_lane + h, ba, 0.0), axis=1, keepdims=True)
    beta = _sigmoid(b_raw)
    g = -jnp.exp(alog_ref[h]) * _softplus(a_raw + dtb_ref[h])
    r3 = lambda x: x.reshape(n, chunk, x.shape[-1])
    u_v, w_k, qk, q_dec, k_dec, g_end = _dn_chunk_terms(r3(q), r3(k), r3(v_ref[...]), r3(beta), r3(g))
    uv_s[...] = u_v.reshape(t, DN_HEAD)
    wk_s[...] = w_k.reshape(t, DN_HEAD)
    qk_s[...] = qk.reshape(t, chunk)
    qd_s[...] = q_dec.reshape(t, DN_HEAD)
    kd_s[...] = k_dec.reshape(t, DN_HEAD)
    ge_s[...] = jnp.broadcast_to(g_end, (n, 1, DN_HEAD))

    def step(c, s):
        rows = pl.ds(pl.multiple_of(c * chunk, chunk), chunk)
        sb = s.astype(BF16)
        u = uv_s[rows, :] - _dot(wk_s[rows, :].astype(BF16), sb)
        ub = u.astype(BF16)
        o = _dot(qd_s[rows, :].astype(BF16), sb) + _dot(qk_s[rows, :].astype(BF16), ub)
        o_ref[rows, :] = _dn_out(o, gn_ref[...], z_ref[rows, :])
        kd = kd_s[rows, :].astype(BF16)
        return ge_s[c] * s + lax.dot_general(kd, ub, (((0,), (0,)), ((), ())), preferred_element_type=F32)

    s_ref[0, 0] = lax.fori_loop(0, n, step, s0_ref[0, 0])


def deltanet_prompt(y, small, z, a_log, dt_bias, out_norm, s0, bsz, seq, nh, ba_block, b_lane, a_lane):
    n = y.shape[0]
    chunk = min(DN_CHUNK, seq)
    d = DN_HEAD
    smem = pl.BlockSpec(memory_space=pltpu.SMEM)
    col = lambda off: pl.BlockSpec((seq, d), lambda b, h, off=off: (b, off + h))
    return pl.pallas_call(
        functools.partial(_dn_prompt_body, chunk=chunk, b_lane=b_lane, a_lane=a_lane),
        grid=(bsz, nh),
        in_specs=[smem, smem, col(0), col(nh), col(2 * nh),
                  pl.BlockSpec((seq, LANES), lambda b, h: (b, ba_block)),
                  pl.BlockSpec((seq, d), lambda b, h: (b, h)),
                  pl.BlockSpec((1, d), lambda b, h: (0, 0)),
                  pl.BlockSpec((1, 1, d, d), lambda b, h: (b, h, 0, 0))],
        out_specs=[pl.BlockSpec((seq, d), lambda b, h: (b, h)),
                   pl.BlockSpec((1, 1, d, d), lambda b, h: (b, h, 0, 0))],
        out_shape=[jax.ShapeDtypeStruct((n, nh * d), BF16), jax.ShapeDtypeStruct((bsz, nh, d, d), F32)],
        scratch_shapes=[pltpu.VMEM((seq, d), F32), pltpu.VMEM((seq, d), F32), pltpu.VMEM((seq, chunk), F32),
                        pltpu.VMEM((seq, d), F32), pltpu.VMEM((seq, d), F32),
                        pltpu.VMEM((seq // chunk, 1, d), F32)],
        compiler_params=_cp("parallel", "parallel"),
        name="deltanet_prompt",
    )(a_log, dt_bias, y, y, y, small, z, out_norm.reshape(1, d), s0)


def _dn_sample_body(q_ref, k_ref, v_ref, b_ref, a_ref, z_ref, alog_ref, dtb_ref, gn_ref, s0_ref, o_ref, s_ref, *,
                    t_real):
    q = _l2n(q_ref[0]) * (DN_HEAD ** -0.5)
    k = _l2n(k_ref[0])
    v = v_ref[0]
    real = lax.broadcasted_iota(jnp.int32, (q.shape[1], 1), 0) < t_real
    beta = jnp.where(real, _sigmoid(b_ref[0][:, :, 0:1]), 0.0)
    g = jnp.where(real, -jnp.exp(alog_ref[...][:, :, 0:1]) * _softplus(a_ref[0][:, :, 0:1] + dtb_ref[...][:, :, 0:1]), 0.0)
    u_v, w_k, qk, q_dec, k_dec, g_end = _dn_chunk_terms(q, k, v, beta, g)
    s = s0_ref[0]
    u = u_v - _bdot(w_k, s, _NN)
    o = _bdot(q_dec, s, _NN) + _bdot(qk, u, _NN)
    o_ref[0] = _dn_out(o, gn_ref[...], z_ref[0])
    s_ref[0] = g_end * s + _bdot(k_dec, u, _TN)


def deltanet_sample(q, k, v, b_raw, a_raw, z, a_log, dt_bias, out_norm, s0, t_real):
    bsz, nh, tp, d = q.shape
    blk = pl.BlockSpec((1, nh, tp, d), lambda b: (b, 0, 0, 0))
    par = pl.BlockSpec((nh, 1, d), lambda b: (0, 0, 0))
    st = pl.BlockSpec((1, nh, d, d), lambda b: (b, 0, 0, 0))
    bc = lambda p: jnp.broadcast_to(p.astype(F32).reshape(nh, 1, 1), (nh, 1, d))
    return pl.pallas_call(
        functools.partial(_dn_sample_body, t_real=t_real),
        grid=(bsz,),
        in_specs=[blk, blk, blk, blk, blk, blk, par, par, pl.BlockSpec((1, d), lambda b: (0, 0)), st],
        out_specs=[blk, st],
        out_shape=[jax.ShapeDtypeStruct((bsz, nh, tp, d), BF16), jax.ShapeDtypeStruct((bsz, nh, d, d), F32)],
        compiler_params=_cp("parallel"),
        name="deltanet_sample",
    )(q, k, v, b_raw, a_raw, z, bc(a_log), bc(dt_bias), out_norm.reshape(1, d), s0)


def _merge_body(oa_ref, od_ref, om_ref, wa_ref, wd_ref, wm_ref, g0_ref, g1_ref, g2_ref, o_ref):
    acc = _sigmoid(g0_ref[...]) * _dot(oa_ref[...], wa_ref[...])
    acc = acc + _sigmoid(g1_ref[...]) * _dot(od_ref[...], wd_ref[...])
    acc = acc + _sigmoid(g2_ref[...]) * _dot(om_ref[...], wm_ref[...])
    o_ref[...] = acc.astype(o_ref.dtype)


def merge_branches(o_mla, o_dn, o_mem, w_mla, w_dn, w_mem, gate_raw, tm, tn):
    n, d = o_mla.shape[0], w_mla.shape[1]
    nj = d // tn
    a_spec = lambda width: pl.BlockSpec((tm, width), lambda i, j: (i, 0))
    w_spec = lambda width: pl.BlockSpec((width, tn), lambda i, j: (0, j))
    g_spec = lambda br: pl.BlockSpec((tm, tn), lambda i, j, br=br: (i, br * nj + j))
    return pl.pallas_call(
        _merge_body,
        grid=(n // tm, nj),
        in_specs=[a_spec(o_mla.shape[1]), a_spec(o_dn.shape[1]), a_spec(o_mem.shape[1]),
                  w_spec(w_mla.shape[0]), w_spec(w_dn.shape[0]), w_spec(w_mem.shape[0]),
                  g_spec(0), g_spec(1), g_spec(2)],
        out_specs=pl.BlockSpec((tm, tn), lambda i, j: (i, j)),
        out_shape=jax.ShapeDtypeStruct((n, d), BF16),
        compiler_params=_cp("parallel", "parallel"),
        name="merge_branches",
    )(o_mla, o_dn, o_mem, w_mla, w_dn, w_mem, gate_raw, gate_raw, gate_raw)


def _rope_tables(pos):
    half = QK_ROPE // 2
    inv_freq = ROPE_BASE ** (-jnp.arange(half, dtype=F32) / half)
    ang = pos.astype(F32)[:, None] * inv_freq[None, :]
    cos, sin = jnp.cos(ang), jnp.sin(ang)
    return jnp.concatenate([cos, cos], axis=1), jnp.concatenate([-sin, sin], axis=1)


PREF = dict(tm=1024, tn=1024, tn_small=512, tn_up=512, tn_down=512, norm_rows=256, mem_rows=512, flash_q=512,
            conv_rows=1024, conv_cols=512, ffn_cols=256, merge_m=512, merge_n=512, pages=8)


def _tile(n, pref, unit=1):
    if n <= pref:
        return n
    for t in range(pref - pref % unit, 0, -unit):
        if n % t == 0:
            return t
    return n


class _LayerWeights:
    def __init__(self, l, w_in, w_uq, w_uk, w_uv, w_mem_kv, w_br_mla, w_br_dn, w_br_mem, w_out, w_up, w_down, dims):
        q_lora, kv_lora, dn_ch, nh_dn, dn_v, mem_d, d_model = dims
        o = 0
        cuts = {}
        for name, width in (("cq", q_lora), ("ckv", kv_lora), ("kpe", QK_ROPE), ("qkv", dn_ch), ("b", nh_dn),
                            ("a", nh_dn), ("z", dn_v), ("mq", mem_d), ("gate", 3 * d_model)):
            cuts[name] = (o, o + width)
            o += width
        w = w_in[l]
        c = lambda a, b: w[:, a:b]
        self.w_small = jnp.concatenate([c(cuts["cq"][0], cuts["kpe"][1]), c(cuts["b"][0], cuts["a"][1])], axis=1).astype(BF16)
        self.w_qkv = c(*cuts["qkv"]).astype(BF16)
        self.w_z = c(*cuts["z"]).astype(BF16)
        self.w_mq = c(*cuts["mq"]).astype(BF16)
        self.w_gate = c(*cuts["gate"]).astype(BF16)
        nh = w_uk.shape[1]
        self.w_uq = jnp.transpose(w_uq[l].reshape(q_lora, nh, -1), (1, 0, 2)).astype(BF16)
        self.w_uk = w_uk[l].astype(BF16)
        self.w_uv = w_uv[l].astype(BF16)
        self.w_mem_kv = w_mem_kv[l].astype(BF16)
        self.w_br_mla = w_br_mla[l].astype(BF16)
        self.w_br_dn = w_br_dn[l].astype(BF16)
        self.w_br_mem = w_br_mem[l].astype(BF16)
        self.w_out = w_out[l].astype(BF16)
        self.w_up = w_up[l].astype(BF16)
        self.w_down = w_down[l].astype(BF16)


def _token_stage(x, wts, norm_mix, tm):
    h = rmsnorm(x, norm_mix, BF16, _tile(x.shape[0], PREF["norm_rows"], SUBLANES))
    mm = lambda w, tn: matmul(h, w, F32, tm, _tile(w.shape[1], tn, LANES))
    tn, tns = PREF["tn"], PREF["tn_small"]
    return mm(wts.w_small, tns), mm(wts.w_qkv, tn), mm(wts.w_z, tn), mm(wts.w_mq, tns), mm(wts.w_gate, tn)


def _ffn_tail(x, merged, wts, norm_ffn, tm):
    x = matmul(merged, wts.w_out, F32, tm, _tile(wts.w_out.shape[1], PREF["tn"], LANES), res=x)
    h = rmsnorm(x, norm_ffn, BF16, _tile(x.shape[0], PREF["norm_rows"], SUBLANES))
    u = matmul(h, wts.w_up, F32, tm, _tile(wts.w_up.shape[1], PREF["tn_up"], LANES))
    return x, u


def _down(act, wts, x, tm):
    d_ff, d_model = wts.w_down.shape
    return matmul(act, wts.w_down, F32, tm, _tile(d_model, PREF["tn_down"], LANES), tk=_tile(d_ff, d_ff // 2, LANES), res=x)


def kernel(x_prompt, x_sample, mem_prompt, cache_ckv, cache_kpe, page_table, cache_mem_k, cache_mem_v, state_dn,
           state_dn_conv, state_ffn_conv, norm_mix, w_in, q_norm, kv_norm, kpe_norm, qh_norm, w_uq, w_uk, w_uv,
           dn_conv_w, dn_a_log, dn_dt_bias, dn_out_norm, mem_norm, w_mem_kv, mk_norm, mq_norm, w_br_mla, w_br_dn,
           w_br_mem, w_out, norm_ffn, w_up, ffn_conv_w, w_down):
    bp, tp, d_model = x_prompt.shape
    bs, ts, _ = x_sample.shape
    depth = w_in.shape[0]
    nh = w_uk.shape[1]
    q_lora = w_uq.shape[1]
    kv_lora = w_uk.shape[3]
    nh_dn = dn_a_log.shape[1]
    dn_v = w_br_dn.shape[1]
    dn_ch = dn_conv_w.shape[2]
    dn_qk = (dn_ch - dn_v) // 2
    mem_tok = mem_prompt.shape[1]
    mem_d = w_br_mem.shape[1]
    d_ff = w_down.shape[1]
    past_len = page_table.shape[1] * cache_ckv.shape[2]
    scale = (QK_NOPE + QK_ROPE) ** -0.5
    dims = (q_lora, kv_lora, dn_ch, nh_dn, dn_v, mem_d, d_model)
    ba_off = q_lora + kv_lora + QK_ROPE
    ba_block, b_lane = divmod(ba_off, LANES)
    a_lane = b_lane + nh_dn
    assert a_lane + nh_dn <= LANES and dn_qk == nh_dn * DN_HEAD

    cos_p, sin_p = _rope_tables(jnp.arange(tp, dtype=jnp.int32))
    cos_s, sin_s = _rope_tables(past_len + jnp.arange(ts, dtype=jnp.int32))
    cos_s, sin_s = jnp.tile(cos_s, (bs, 1)), jnp.tile(sin_s, (bs, 1))

    n_p, n_s = bp * tp, bs * ts
    tm_p = _tile(n_p, PREF["tm"], SUBLANES)
    tm_s = _tile(n_s, PREF["tm"], SUBLANES)
    conv_tm = _tile(tp, PREF["conv_rows"], SUBLANES)
    mem_tm = _tile(bp * mem_tok, PREF["mem_rows"], SUBLANES)
    tp_pad = SUBLANES * -(-ts // SUBLANES)
    xp = x_prompt.reshape(n_p, d_model)
    xs = x_sample.reshape(n_s, d_model)
    mem_rows = mem_prompt.reshape(bp * mem_tok, d_model)
    zeros_state = jnp.zeros((bp, nh_dn, DN_HEAD, DN_HEAD), F32)
    zeros_dn_hist = jnp.zeros((bp, dn_conv_w.shape[1] - 1, dn_ch), F32)
    zeros_ffn_hist = jnp.zeros((bp, ffn_conv_w.shape[1] - 1, 2 * d_ff), F32)

    outs_p = {k: [] for k in ("ckv", "kpe", "dns", "dnh", "ffh", "mk", "mv")}
    outs_s = {k: [] for k in ("ckv", "kpe", "dns", "dnh", "ffh")}

    for l in range(depth):
        wts = _LayerWeights(l, w_in, w_uq, w_uk, w_uv, w_mem_kv, w_br_mla, w_br_dn, w_br_mem, w_out, w_up, w_down, dims)
        dn_w = dn_conv_w[l]
        ffn_w = ffn_conv_w[l]
        dn_wd = dn_w.shape[0]
        ffn_wd = ffn_w.shape[0]

        mem_h = rmsnorm(mem_rows, mem_norm[l], BF16, _tile(bp * mem_tok, PREF["norm_rows"], SUBLANES))
        mem_kv = matmul(mem_h, wts.w_mem_kv, F32, mem_tm, _tile(2 * mem_d, PREF["tn_small"], LANES))
        mk_p = headnorm(mem_kv[:, :mem_d], mk_norm[l], MEM_HD, mem_tm)
        mv_p = mem_kv[:, mem_d:]

        small, qkv, z, mq, gate_raw = _token_stage(xp, wts, norm_mix[l], tm_p)
        ql, qr, ckv, ckvb, kpe, kpeb = mla_prep(small, q_norm[l], kv_norm[l], kpe_norm[l], qh_norm[l], wts.w_uq,
                                                wts.w_uk, cos_p, sin_p, _tile(tp, PREF["tm"], SUBLANES), scale)
        o_lat = flash_prompt(ql, qr, ckvb, kpeb, bp, tp, _tile(tp, PREF["flash_q"], SUBLANES))
        o_mla = uv_project(o_lat, wts.w_uv, tm_p)
        y = conv_silu_rows(qkv, zeros_dn_hist, dn_w, tp, conv_tm, _tile(dn_ch, PREF["conv_cols"], LANES))
        o_dn, dns_p = deltanet_prompt(y, small, z, dn_a_log[l], dn_dt_bias[l], dn_out_norm[l], zeros_state,
                                      bp, tp, nh_dn, ba_block, b_lane, a_lane)
        o_mem = mem_attend(mq.reshape(bp, tp, mem_d), mk_p.reshape(bp, mem_tok, mem_d),
                           mv_p.reshape(bp, mem_tok, mem_d), mq_norm[l], _tile(tp, PREF["flash_q"], SUBLANES)).reshape(n_p, mem_d)
        merged = merge_branches(o_mla, o_dn, o_mem, wts.w_br_mla, wts.w_br_dn, wts.w_br_mem, gate_raw,
                                _tile(n_p, PREF["merge_m"], SUBLANES), _tile(d_model, PREF["merge_n"], LANES))
        xp, u = _ffn_tail(xp, merged, wts, norm_ffn[l], tm_p)
        act = conv_swiglu_rows(u, zeros_ffn_hist, ffn_w, tp, conv_tm, _tile(d_ff, PREF["ffn_cols"], LANES))
        xp = _down(act, wts, xp, tm_p)

        outs_p["ckv"].append(ckv.reshape(bp, tp, kv_lora))
        outs_p["kpe"].append(kpe.reshape(bp, tp, QK_ROPE))
        outs_p["dns"].append(dns_p)
        outs_p["dnh"].append(qkv.reshape(bp, tp, dn_ch)[:, tp - (dn_wd - 1):])
        outs_p["ffh"].append(u.reshape(bp, tp, 2 * d_ff)[:, tp - (ffn_wd - 1):])
        outs_p["mk"].append(mk_p.reshape(bp, mem_tok, mem_d // MEM_HD, MEM_HD))
        outs_p["mv"].append(mv_p.reshape(bp, mem_tok, mem_d // MEM_HD, MEM_HD))

        small, qkv, z, mq, gate_raw = _token_stage(xs, wts, norm_mix[l], tm_s)
        ql, qr, ckv, ckvb, kpe, kpeb = mla_prep(small, q_norm[l], kv_norm[l], kpe_norm[l], qh_norm[l], wts.w_uq,
                                                wts.w_uk, cos_s, sin_s, tm_s, scale)
        to_bht = lambda a: jnp.transpose(a.reshape(nh, bs, ts, -1), (1, 0, 2, 3)).reshape(bs, nh * ts, -1)
        o_lat = paged_sample(to_bht(ql), to_bht(qr), ckv.reshape(bs, ts, kv_lora), kpe.reshape(bs, ts, QK_ROPE),
                             cache_ckv, cache_kpe, page_table, l, _tile(page_table.shape[1], PREF["pages"]))
        o_lat = jnp.transpose(o_lat.reshape(bs, nh, ts, kv_lora), (1, 0, 2, 3)).reshape(nh, n_s, kv_lora)
        o_mla = uv_project(o_lat, wts.w_uv, tm_s)

        hist = state_dn_conv[l]
        qkv_tm = jnp.transpose(qkv.reshape(bs, ts, dn_ch), (1, 0, 2))
        y_tm = conv_silu_tm(qkv_tm, jnp.transpose(hist, (1, 0, 2)), dn_w, _tile(dn_ch, PREF["conv_cols"], LANES))
        pad_t = lambda a: jnp.pad(a, ((0, 0), (0, 0), (0, tp_pad - ts), (0, 0)))
        y5 = pad_t(jnp.transpose(y_tm.reshape(ts, bs, 3 * nh_dn, DN_HEAD), (1, 2, 0, 3)))
        z4 = pad_t(jnp.transpose(z.reshape(bs, ts, nh_dn, DN_HEAD), (0, 2, 1, 3)))
        ba = jnp.transpose(small[:, ba_off:ba_off + 2 * nh_dn].reshape(bs, ts, 2 * nh_dn), (0, 2, 1))[..., None]
        ba = pad_t(jnp.broadcast_to(ba, (bs, 2 * nh_dn, ts, DN_HEAD)))
        o_dn, dns_s = deltanet_sample(y5[:, :nh_dn], y5[:, nh_dn:2 * nh_dn], y5[:, 2 * nh_dn:], ba[:, :nh_dn],
                                      ba[:, nh_dn:], z4, dn_a_log[l], dn_dt_bias[l], dn_out_norm[l], state_dn[l], ts)
        o_dn = jnp.transpose(o_dn[:, :, :ts], (0, 2, 1, 3)).reshape(n_s, dn_v)
        dnh_s = jnp.concatenate([hist, qkv.reshape(bs, ts, dn_ch)], axis=1)[:, ts:]

        mq_pad = jnp.pad(mq.reshape(bs, ts, mem_d), ((0, 0), (0, tp_pad - ts), (0, 0)))
        o_mem = mem_attend(mq_pad, cache_mem_k[l].reshape(bs, mem_tok, mem_d), cache_mem_v[l].reshape(bs, mem_tok, mem_d),
                           mq_norm[l], tp_pad)[:, :ts].reshape(n_s, mem_d)
        merged = merge_branches(o_mla, o_dn, o_mem, wts.w_br_mla, wts.w_br_dn, wts.w_br_mem, gate_raw,
                                _tile(n_s, PREF["merge_m"], SUBLANES), _tile(d_model, PREF["merge_n"], LANES))
        xs, u = _ffn_tail(xs, merged, wts, norm_ffn[l], tm_s)
        fh = state_ffn_conv[l]
        u3 = u.reshape(bs, ts, 2 * d_ff)
        act = conv_swiglu_tm(jnp.transpose(u3, (1, 0, 2)), jnp.transpose(fh, (1, 0, 2)), ffn_w,
                             _tile(d_ff, PREF["ffn_cols"], LANES))
        act = jnp.transpose(act, (1, 0, 2)).reshape(n_s, d_ff)
        xs = _down(act, wts, xs, tm_s)

        outs_s["ckv"].append(ckv.reshape(bs, ts, kv_lora))
        outs_s["kpe"].append(kpe.reshape(bs, ts, QK_ROPE))
        outs_s["dns"].append(dns_s)
        outs_s["dnh"].append(dnh_s)
        outs_s["ffh"].append(jnp.concatenate([fh, u3], axis=1)[:, ts:])

    st = lambda d, k: jnp.stack(d[k])
    return (xp.reshape(bp, tp, d_model), xs.reshape(bs, ts, d_model),
            st(outs_p, "ckv"), st(outs_p, "kpe"), st(outs_p, "dns"), st(outs_p, "dnh"), st(outs_p, "ffh"),
            st(outs_p, "mk"), st(outs_p, "mv"),
            st(outs_s, "ckv"), st(outs_s, "kpe"), st(outs_s, "dns"), st(outs_s, "dnh"), st(outs_s, "ffh"))
```

```python
import functools
import math

import jax
import jax.numpy as jnp
from jax import lax
from jax.experimental import pallas as pl
from jax.experimental.pallas import tpu as pltpu

F32 = jnp.float32
BF16 = jnp.bfloat16
EPS = 1e-6
ROPE_BASE = 10000.0
LANES = 128
SUBLANES = 8
VMEM_LIMIT_BYTES = 56 * 1024 * 1024

DN_CHUNK = 64
DN_HEAD = 128
MEM_HD = 128
QK_NOPE = 128
QK_ROPE = 64


def _cp(*sem):
    return pltpu.CompilerParams(dimension_semantics=sem, vmem_limit_bytes=VMEM_LIMIT_BYTES)


def _rms(x, g):
    return x * lax.rsqrt(jnp.mean(x * x, axis=-1, keepdims=True) + EPS) * g


def _silu(x):
    return x * (1.0 / (1.0 + jnp.exp(-x)))


def _sigmoid(x):
    return 1.0 / (1.0 + jnp.exp(-x))


def _softplus(x):
    return jnp.maximum(x, 0.0) + jnp.log(1.0 + jnp.exp(-jnp.abs(x)))


def _dot(a, b):
    return jnp.dot(a, b, preferred_element_type=F32)


def _dot_nt(a, b):
    return lax.dot_general(a, b, (((1,), (1,)), ((), ())), preferred_element_type=F32)


def _rmsnorm_body(x_ref, g_ref, o_ref):
    o_ref[...] = _rms(x_ref[...].astype(F32), g_ref[...]).astype(o_ref.dtype)


def rmsnorm(x, g, out_dtype, tm):
    n, d = x.shape
    return pl.pallas_call(
        _rmsnorm_body,
        grid=(n // tm,),
        in_specs=[pl.BlockSpec((tm, d), lambda i: (i, 0)), pl.BlockSpec((1, d), lambda i: (0, 0))],
        out_specs=pl.BlockSpec((tm, d), lambda i: (i, 0)),
        out_shape=jax.ShapeDtypeStruct((n, d), out_dtype),
        compiler_params=_cp("parallel"),
        name="rmsnorm",
    )(x, g.reshape(1, d).astype(F32))


def _headnorm_body(x_ref, g_ref, o_ref, *, hd):
    x = x_ref[...]
    for h in range(x.shape[1] // hd):
        o_ref[:, h * hd:(h + 1) * hd] = _rms(x[:, h * hd:(h + 1) * hd], g_ref[...])


def headnorm(x, g, hd, tm):
    n, d = x.shape
    return pl.pallas_call(
        functools.partial(_headnorm_body, hd=hd),
        grid=(n // tm,),
        in_specs=[pl.BlockSpec((tm, d), lambda i: (i, 0)), pl.BlockSpec((1, hd), lambda i: (0, 0))],
        out_specs=pl.BlockSpec((tm, d), lambda i: (i, 0)),
        out_shape=jax.ShapeDtypeStruct((n, d), F32),
        compiler_params=_cp("parallel"),
        name="headnorm",
    )(x, g.reshape(1, hd).astype(F32))


def _mm_body(*refs, nk, has_res):
    if has_res:
        a_ref, b_ref, r_ref, o_ref = refs[:4]
    else:
        a_ref, b_ref, o_ref = refs[:3]
        r_ref = None
    if nk == 1:
        acc = _dot(a_ref[...], b_ref[...])
        if has_res:
            acc = r_ref[...] + acc
        o_ref[...] = acc.astype(o_ref.dtype)
        return
    acc_ref = refs[-1]
    k = pl.program_id(2)

    @pl.when(k == 0)
    def _():
        acc_ref[...] = jnp.zeros_like(acc_ref)

    acc_ref[...] += _dot(a_ref[...], b_ref[...])

    @pl.when(k == nk - 1)
    def _():
        acc = acc_ref[...]
        if has_res:
            acc = r_ref[...] + acc
        o_ref[...] = acc.astype(o_ref.dtype)


def matmul(a, b, out_dtype, tm, tn, tk=None, res=None, layer=0):
    m, kk = a.shape
    n = b.shape[1]
    tk = kk if tk is None else tk
    nk = kk // tk
    assert m % tm == 0 and n % tn == 0 and kk % tk == 0 and b.shape[0] % kk == 0
    in_specs = [pl.BlockSpec((tm, tk), lambda i, j, k: (i, k)),
                pl.BlockSpec((tk, tn), lambda i, j, k: (layer * nk + k, j))]
    args = [a, b]
    if res is not None:
        in_specs.append(pl.BlockSpec((tm, tn), lambda i, j, k: (i, j)))
        args.append(res)
    return pl.pallas_call(
        functools.partial(_mm_body, nk=nk, has_res=res is not None),
        grid=(m // tm, n // tn, nk),
        in_specs=in_specs,
        out_specs=pl.BlockSpec((tm, tn), lambda i, j, k: (i, j)),
        out_shape=jax.ShapeDtypeStruct((m, n), out_dtype),
        scratch_shapes=[pltpu.VMEM((tm, tn), F32)] if nk > 1 else [],
        compiler_params=_cp("parallel", "parallel", "arbitrary"),
        name="matmul",
    )(*args)


def _rope(x, cos2, sin2):
    half = x.shape[1] // 2
    rot = jnp.concatenate([x[:, half:], x[:, :half]], axis=1)
    return x * cos2 + rot * sin2


def _mla_prep_body(small_ref, qn_ref, kvn_ref, kpn_ref, qhn_ref, wuq_ref, wuk_ref, cos_ref, sin_ref,
                   ql_ref, qr_ref, ckv_ref, ckvb_ref, kpe_ref, kpeb_ref, cqn_ref, *, q_lora, kv_lora, scale):
    h = pl.program_id(1)

    @pl.when(h == 0)
    def _():
        cqn_ref[...] = _rms(small_ref[:, 0:q_lora], qn_ref[...]).astype(BF16)
        ckv = _rms(small_ref[:, q_lora:q_lora + kv_lora], kvn_ref[...])
        ckv_ref[...] = ckv
        ckvb_ref[...] = ckv.astype(BF16)
        o = q_lora + kv_lora
        kpe = _rope(_rms(small_ref[:, o:o + QK_ROPE], kpn_ref[...]), cos_ref[...], sin_ref[...])
        kpe_ref[...] = kpe
        kpeb_ref[...] = kpe.astype(BF16)

    q = _rms(_dot(cqn_ref[...], wuq_ref[0]), qhn_ref[...])
    ql = _dot(q[:, :QK_NOPE].astype(BF16), wuk_ref[0]) * scale
    ql_ref[0] = ql.astype(BF16)
    qr_ref[0] = (_rope(q[:, QK_NOPE:], cos_ref[...], sin_ref[...]) * scale).astype(BF16)


def mla_prep(small, q_norm, kv_norm, kpe_norm, qh_norm, wuq_h, wuk, cos2, sin2, tm, scale, nh, layer):
    n = small.shape[0]
    _, q_lora, hq = wuq_h.shape
    kv_lora = wuk.shape[2]
    h0 = layer * nh
    nrep = cos2.shape[0] // tm
    row = lambda i, h: (i, 0)
    const = lambda i, h: (0, 0)
    tab = lambda i, h: (i % nrep, 0)
    return pl.pallas_call(
        functools.partial(_mla_prep_body, q_lora=q_lora, kv_lora=kv_lora, scale=scale),
        grid=(n // tm, nh),
        in_specs=[
            pl.BlockSpec((tm, small.shape[1]), row),
            pl.BlockSpec((1, q_lora), const), pl.BlockSpec((1, kv_lora), const),
            pl.BlockSpec((1, QK_ROPE), const), pl.BlockSpec((1, hq), const),
            pl.BlockSpec((1, q_lora, hq), lambda i, h: (h0 + h, 0, 0)),
            pl.BlockSpec((1, QK_NOPE, kv_lora), lambda i, h: (h0 + h, 0, 0)),
            pl.BlockSpec((tm, QK_ROPE), tab), pl.BlockSpec((tm, QK_ROPE), tab),
        ],
        out_specs=[
            pl.BlockSpec((1, tm, kv_lora), lambda i, h: (h, i, 0)),
            pl.BlockSpec((1, tm, QK_ROPE), lambda i, h: (h, i, 0)),
            pl.BlockSpec((tm, kv_lora), row), pl.BlockSpec((tm, kv_lora), row),
            pl.BlockSpec((tm, QK_ROPE), row), pl.BlockSpec((tm, QK_ROPE), row),
        ],
        out_shape=[
            jax.ShapeDtypeStruct((nh, n, kv_lora), BF16), jax.ShapeDtypeStruct((nh, n, QK_ROPE), BF16),
            jax.ShapeDtypeStruct((n, kv_lora), F32), jax.ShapeDtypeStruct((n, kv_lora), BF16),
            jax.ShapeDtypeStruct((n, QK_ROPE), F32), jax.ShapeDtypeStruct((n, QK_ROPE), BF16),
        ],
        scratch_shapes=[pltpu.VMEM((tm, q_lora), BF16)],
        compiler_params=_cp("parallel", "arbitrary"),
        name="mla_prep",
    )(small, q_norm.reshape(1, -1), kv_norm.reshape(1, -1), kpe_norm.reshape(1, -1), qh_norm.reshape(1, -1),
      wuq_h, wuk, cos2, sin2)


def _flash_body(ql_ref, qr_ref, ckv_ref, kpe_ref, o_ref, m_ref, l_ref, acc_ref, *, tq, hb):
    qi = pl.program_id(1)
    m_ref[...] = jnp.full_like(m_ref, -jnp.inf)
    l_ref[...] = jnp.zeros_like(l_ref)
    acc_ref[...] = jnp.zeros_like(acc_ref)

    def scores(j):
        ks = pl.multiple_of(j * tq, tq)
        kc = ckv_ref[pl.ds(ks, tq), :]
        kp = kpe_ref[pl.ds(ks, tq), :]
        return [_dot_nt(ql_ref[hh], kc) + _dot_nt(qr_ref[hh], kp) for hh in range(hb)]

    def fold(j, ss, masked):
        kc = ckv_ref[pl.ds(pl.multiple_of(j * tq, tq), tq), :]
        for hh, s in enumerate(ss):
            if masked:
                r = lax.broadcasted_iota(jnp.int32, (tq, tq), 0)
                c = lax.broadcasted_iota(jnp.int32, (tq, tq), 1)
                s = jnp.where(c <= r, s, -jnp.inf)
            m_old = m_ref[hh]
            m_new = jnp.maximum(m_old, jnp.max(s, axis=1, keepdims=True))
            p = jnp.exp(s - m_new)
            corr = jnp.exp(m_old - m_new)
            l_ref[hh] = l_ref[hh] * corr + jnp.sum(p, axis=1, keepdims=True)
            acc_ref[hh] = acc_ref[hh] * corr + _dot(p.astype(BF16), kc)
            m_ref[hh] = m_new

    def body(j, ss):
        nxt = scores(j + 1)
        fold(j, ss, False)
        return nxt

    fold(qi, lax.fori_loop(0, qi, body, scores(0)), True)
    o_ref[...] = (acc_ref[...] / l_ref[...]).astype(o_ref.dtype)


def flash_prompt(ql, qr, ckvb, kpeb, bsz, seq, tq, hb):
    nh, n, kv_lora = ql.shape
    nq = seq // tq
    assert nh % hb == 0
    qmap = lambda b, qi, h: (h, b * nq + qi, 0)
    kmap = lambda b, qi, h: (b, 0)
    return pl.pallas_call(
        functools.partial(_flash_body, tq=tq, hb=hb),
        grid=(bsz, nq, nh // hb),
        in_specs=[
            pl.BlockSpec((hb, tq, kv_lora), qmap), pl.BlockSpec((hb, tq, QK_ROPE), qmap),
            pl.BlockSpec((seq, kv_lora), kmap), pl.BlockSpec((seq, QK_ROPE), kmap),
        ],
        out_specs=pl.BlockSpec((hb, tq, kv_lora), qmap),
        out_shape=jax.ShapeDtypeStruct((nh, n, kv_lora), BF16),
        scratch_shapes=[pltpu.VMEM((hb, tq, 1), F32), pltpu.VMEM((hb, tq, 1), F32), pltpu.VMEM((hb, tq, kv_lora), F32)],
        compiler_params=_cp("parallel", "parallel", "arbitrary"),
        name="flash_prompt",
    )(ql, qr, ckvb, kpeb)


def _paged_body(pt_ref, ql_ref, qr_ref, cn_ref, pn_ref, *refs, pg, sub, t_new):
    ck_refs = refs[:pg]
    kp_refs = refs[pg:2 * pg]
    o_ref, m_ref, l_ref, acc_ref = refs[2 * pg:]
    p = pl.program_id(1)
    ql = ql_ref[0]
    qr = qr_ref[0]

    @pl.when(p == 0)
    def _():
        cn = cn_ref[0].astype(BF16)
        s = _dot_nt(ql, cn) + _dot_nt(qr, pn_ref[0].astype(BF16))
        r = lax.rem(lax.broadcasted_iota(jnp.int32, s.shape, 0), t_new)
        c = lax.broadcasted_iota(jnp.int32, s.shape, 1)
        s = jnp.where(c <= r, s, -jnp.inf)
        m = jnp.max(s, axis=1, keepdims=True)
        pr = jnp.exp(s - m)
        m_ref[...] = m
        l_ref[...] = jnp.sum(pr, axis=1, keepdims=True)
        acc_ref[...] = _dot(pr.astype(BF16), cn)

    m_run = m_ref[...]
    l_run = l_ref[...]
    acc = acc_ref[...]
    for g0 in range(0, pg, sub):
        js = range(g0, g0 + sub)
        kcs = [ck_refs[j][0, 0].astype(BF16) for j in js]
        ss = [_dot_nt(ql, kc) + _dot(qr, kp_refs[j][0, 0].astype(BF16)) for j, kc in zip(js, kcs)]
        m_new = m_run
        for s in ss:
            m_new = jnp.maximum(m_new, jnp.max(s, axis=1, keepdims=True))
        corr = jnp.exp(m_run - m_new)
        l_run = l_run * corr
        acc = acc * corr
        for s, kc in zip(ss, kcs):
            pr = jnp.exp(s - m_new)
            l_run = l_run + jnp.sum(pr, axis=1, keepdims=True)
            acc = acc + _dot(pr.astype(BF16), kc)
        m_run = m_new
    m_ref[...] = m_run
    l_ref[...] = l_run
    acc_ref[...] = acc

    @pl.when(p == pl.num_programs(1) - 1)
    def _():
        o_ref[0] = (acc_ref[...] / l_ref[...]).astype(o_ref.dtype)


def paged_sample(ql, qr, ckv_new, kpe_new, cache_ckv, cache_kpe_t, page_table, layer, pg, sub):
    bsz, rows, kv_lora = ql.shape
    t_new = ckv_new.shape[1]
    n_pages = page_table.shape[1]
    page = cache_ckv.shape[2]
    assert n_pages % pg == 0 and pg % sub == 0
    fixed = lambda b, p, pt: (b, 0, 0)

    def page_map(j):
        return lambda b, p, pt: (layer, pt[b, p * pg + j], 0, 0)

    in_specs = [
        pl.BlockSpec((1, rows, kv_lora), fixed), pl.BlockSpec((1, rows, QK_ROPE), fixed),
        pl.BlockSpec((1, t_new, kv_lora), fixed), pl.BlockSpec((1, t_new, QK_ROPE), fixed),
    ]
    in_specs += [pl.BlockSpec((1, 1, page, kv_lora), page_map(j)) for j in range(pg)]
    in_specs += [pl.BlockSpec((1, 1, QK_ROPE, page), page_map(j)) for j in range(pg)]
    return pl.pallas_call(
        functools.partial(_paged_body, pg=pg, sub=sub, t_new=t_new),
        grid_spec=pltpu.PrefetchScalarGridSpec(
            num_scalar_prefetch=1,
            grid=(bsz, n_pages // pg),
            in_specs=in_specs,
            out_specs=pl.BlockSpec((1, rows, kv_lora), fixed),
            scratch_shapes=[pltpu.VMEM((rows, 1), F32), pltpu.VMEM((rows, 1), F32), pltpu.VMEM((rows, kv_lora), F32)],
        ),
        out_shape=jax.ShapeDtypeStruct((bsz, rows, kv_lora), BF16),
        compiler_params=_cp("parallel", "arbitrary"),
        name="paged_sample",
    )(page_table, ql, qr, ckv_new, kpe_new, *([cache_ckv] * pg), *([cache_kpe_t] * pg))


def _uv_body(o_ref, w_ref, out_ref):
    out_ref[...] = _dot(o_ref[0], w_ref[0]).astype(out_ref.dtype)


def uv_project(o_lat, wuv, tm, layer):
    nh, n, kv_lora = o_lat.shape
    vh = wuv.shape[2]
    h0 = layer * nh
    return pl.pallas_call(
        _uv_body,
        grid=(n // tm, nh),
        in_specs=[pl.BlockSpec((1, tm, kv_lora), lambda i, h: (h, i, 0)),
                  pl.BlockSpec((1, kv_lora, vh), lambda i, h: (h0 + h, 0, 0))],
        out_specs=pl.BlockSpec((tm, vh), lambda i, h: (i, h)),
        out_shape=jax.ShapeDtypeStruct((n, nh * vh), BF16),
        compiler_params=_cp("parallel", "parallel"),
        name="uv_project",
    )(o_lat, wuv)


def _mem_attn_body(q_ref, k_ref, v_ref, g_ref, o_ref, *, nh):
    q = q_ref[0]
    for h in range(nh):
        sl = slice(h * MEM_HD, (h + 1) * MEM_HD)
        qh = _rms(q[:, sl], g_ref[...]).astype(BF16)
        s = _dot_nt(qh, k_ref[0, :, sl].astype(BF16)) * (MEM_HD ** -0.5)
        m = jnp.max(s, axis=1, keepdims=True)
        p = jnp.exp(s - m)
        l = jnp.sum(p, axis=1, keepdims=True)
        o = _dot(p.astype(BF16), v_ref[0, :, sl].astype(BF16)) / l
        o_ref[0, :, sl] = o.astype(o_ref.dtype)


def mem_attend(q, k, v, mq_norm, tq, kv_first=0):
    bsz, t, d = q.shape
    m = k.shape[1]
    return pl.pallas_call(
        functools.partial(_mem_attn_body, nh=d // MEM_HD),
        grid=(bsz, t // tq),
        in_specs=[pl.BlockSpec((1, tq, d), lambda b, i: (b, i, 0)),
                  pl.BlockSpec((1, m, d), lambda b, i: (kv_first + b, 0, 0)),
                  pl.BlockSpec((1, m, d), lambda b, i: (kv_first + b, 0, 0)),
                  pl.BlockSpec((1, MEM_HD), lambda b, i: (0, 0))],
        out_specs=pl.BlockSpec((1, tq, d), lambda b, i: (b, i, 0)),
        out_shape=jax.ShapeDtypeStruct((bsz, t, d), BF16),
        compiler_params=_cp("parallel", "parallel"),
        name="mem_attend",
    )(q, k, v, mq_norm.reshape(1, MEM_HD))


def _conv_rows(x_ref, prev_ref, hist_ref, w_ref, xs_ref, is_start, width):
    tm = x_ref.shape[0]
    xs_ref[0:SUBLANES, :] = jnp.where(is_start, hist_ref[0], prev_ref[...])
    xs_ref[SUBLANES:, :] = x_ref[...]
    y = xs_ref[pl.ds(SUBLANES, tm), :] * w_ref[width - 1:width, :]
    for i in range(width - 1):
        y = y + xs_ref[pl.ds(SUBLANES - (width - 1) + i, tm), :] * w_ref[i:i + 1, :]
    return y


def _conv_silu_body(x_ref, prev_ref, hist_ref, w_ref, o_ref, xs_ref, *, width, tiles_per_seq):
    is_start = pl.program_id(1) % tiles_per_seq == 0
    o_ref[...] = _silu(_conv_rows(x_ref, prev_ref, hist_ref, w_ref, xs_ref, is_start, width))


def _conv_swiglu_body(g_ref, gp_ref, gh_ref, gw_ref, u_ref, up_ref, uh_ref, uw_ref, o_ref, gs_ref, us_ref, *,
                      width, tiles_per_seq):
    is_start = pl.program_id(1) % tiles_per_seq == 0
    gate = _conv_rows(g_ref, gp_ref, gh_ref, gw_ref, gs_ref, is_start, width)
    up = _conv_rows(u_ref, up_ref, uh_ref, uw_ref, us_ref, is_start, width)
    o_ref[...] = (_silu(gate) * up).astype(o_ref.dtype)


def _hist8(hist):
    return jnp.pad(hist, ((0, 0), (SUBLANES - hist.shape[1], 0), (0, 0)))


def conv_silu_rows(x, hist, w, seq, tm, tc):
    n, c = x.shape
    width = w.shape[0]
    tps = seq // tm
    r8 = tm // SUBLANES
    return pl.pallas_call(
        functools.partial(_conv_silu_body, width=width, tiles_per_seq=tps),
        grid=(c // tc, n // tm),
        in_specs=[pl.BlockSpec((tm, tc), lambda j, i: (i, j)),
                  pl.BlockSpec((SUBLANES, tc), lambda j, i: (jnp.maximum(i * r8 - 1, 0), j)),
                  pl.BlockSpec((1, SUBLANES, tc), lambda j, i: (i // tps, 0, j)),
                  pl.BlockSpec((width, tc), lambda j, i: (0, j))],
        out_specs=pl.BlockSpec((tm, tc), lambda j, i: (i, j)),
        out_shape=jax.ShapeDtypeStruct((n, c), F32),
        scratch_shapes=[pltpu.VMEM((tm + SUBLANES, tc), F32)],
        compiler_params=_cp("parallel", "parallel"),
        name="conv_silu_rows",
    )(x, x, _hist8(hist), w)


def conv_swiglu_rows(u, hist, w, seq, tm, tc):
    n, c2 = u.shape
    f = c2 // 2
    width = w.shape[0]
    tps = seq // tm
    r8 = tm // SUBLANES
    nf = f // tc
    h8 = _hist8(hist)

    def specs(off):
        return [pl.BlockSpec((tm, tc), lambda j, i: (i, j + off)),
                pl.BlockSpec((SUBLANES, tc), lambda j, i: (jnp.maximum(i * r8 - 1, 0), j + off)),
                pl.BlockSpec((1, SUBLANES, tc), lambda j, i: (i // tps, 0, j + off)),
                pl.BlockSpec((width, tc), lambda j, i: (0, j + off))]

    return pl.pallas_call(
        functools.partial(_conv_swiglu_body, width=width, tiles_per_seq=tps),
        grid=(nf, n // tm),
        in_specs=specs(0) + specs(nf),
        out_specs=pl.BlockSpec((tm, tc), lambda j, i: (i, j)),
        out_shape=jax.ShapeDtypeStruct((n, f), BF16),
        scratch_shapes=[pltpu.VMEM((tm + SUBLANES, tc), F32), pltpu.VMEM((tm + SUBLANES, tc), F32)],
        compiler_params=_cp("parallel", "parallel"),
        name="conv_swiglu_rows",
    )(u, u, h8, w, u, u, h8, w)


def _conv_tm(x_ref, h_ref, w_ref, width):
    t = x_ref.shape[0]
    rows = [h_ref[i] for i in range(width - 1)] + [x_ref[i] for i in range(t)]
    out = []
    for s in range(t):
        y = rows[s] * w_ref[0:1, :]
        for i in range(1, width):
            y = y + rows[s + i] * w_ref[i:i + 1, :]
        out.append(y)
    return out


def _conv_silu_tm_body(x_ref, h_ref, w_ref, o_ref, *, width):
    for s, y in enumerate(_conv_tm(x_ref, h_ref, w_ref, width)):
        o_ref[s] = _silu(y)


def _conv_swiglu_tm_body(g_ref, gh_ref, gw_ref, u_ref, uh_ref, uw_ref, o_ref, *, width):
    gate = _conv_tm(g_ref, gh_ref, gw_ref, width)
    up = _conv_tm(u_ref, uh_ref, uw_ref, width)
    for s in range(len(gate)):
        o_ref[s] = (_silu(gate[s]) * up[s]).astype(o_ref.dtype)


def conv_silu_tm(x, hist, w, tc):
    t, b, c = x.shape
    width = w.shape[0]
    return pl.pallas_call(
        functools.partial(_conv_silu_tm_body, width=width),
        grid=(c // tc,),
        in_specs=[pl.BlockSpec((t, b, tc), lambda j: (0, 0, j)),
                  pl.BlockSpec((width - 1, b, tc), lambda j: (0, 0, j)),
                  pl.BlockSpec((width, tc), lambda j: (0, j))],
        out_specs=pl.BlockSpec((t, b, tc), lambda j: (0, 0, j)),
        out_shape=jax.ShapeDtypeStruct((t, b, c), F32),
        compiler_params=_cp("parallel"),
        name="conv_silu_tm",
    )(x, hist, w)


def conv_swiglu_tm(u, hist, w, tc):
    t, b, c2 = u.shape
    f = c2 // 2
    nf = f // tc
    width = w.shape[0]

    def specs(off):
        return [pl.BlockSpec((t, b, tc), lambda j: (0, 0, j + off)),
                pl.BlockSpec((width - 1, b, tc), lambda j: (0, 0, j + off)),
                pl.BlockSpec((width, tc), lambda j: (0, j + off))]

    return pl.pallas_call(
        functools.partial(_conv_swiglu_tm_body, width=width),
        grid=(nf,),
        in_specs=specs(0) + specs(nf),
        out_specs=pl.BlockSpec((t, b, tc), lambda j: (0, 0, j)),
        out_shape=jax.ShapeDtypeStruct((t, b, f), BF16),
        compiler_params=_cp("parallel"),
        name="conv_swiglu_tm",
    )(u, hist, w, u, hist, w)


_NN = ((2,), (1,))
_NT = ((2,), (2,))
_TN = ((1,), (1,))


def _bdot(a, b, dims):
    return lax.dot_general(a.astype(BF16), b.astype(BF16), (dims, ((0,), (0,))), preferred_element_type=F32)


def _split_bf16(x, terms):
    parts, r = [], x
    for _ in range(terms):
        p = r.astype(BF16)
        parts.append(p)
        r = r - p.astype(F32)
    return parts


def _l2n(x):
    return x * lax.rsqrt(jnp.sum(x * x, axis=-1, keepdims=True) + EPS)


def _dn_chunk_terms(q, k, v, beta, g):
    n, c, _ = q.shape
    ii = lax.broadcasted_iota(jnp.int32, (c, c), 0)
    jj = lax.broadcasted_iota(jnp.int32, (c, c), 1)
    incl = jj <= ii
    gm = jnp.broadcast_to(g, (n, c, c))
    ltri = jnp.broadcast_to(incl.astype(BF16)[None], (n, c, c))
    ones = jnp.ones((n, c, c), BF16)
    gc = sum(_bdot(ltri, p, _NN) for p in _split_bf16(gm, 3))
    gr = sum(_bdot(ones, p, _NN) for p in _split_bf16(jnp.where(ii <= jj, gm, 0.0), 3))
    decay = jnp.where(incl, jnp.exp(gc - gr), 0.0)
    gcol = gc[:, :, 0:1]
    eg = jnp.exp(gcol)
    gend = gc[:, c - 1:c, 0:1]
    k_hi, k_lo = _split_bf16(k, 2)
    kk = _bdot(k_hi, k_hi, _NT) + (_bdot(k_hi, k_lo, _NT) + _bdot(k_lo, k_hi, _NT))
    lo = jnp.where(jj < ii, beta * kk * decay, 0.0)
    inv = jnp.where(ii == jj, 1.0, 0.0) - lo
    pw = lo
    for _ in range(int(math.log2(c)) - 1):
        pw = _bdot(pw, pw, _NN)
        inv = inv + _bdot(inv, pw, _NN)
    u_v = _bdot(inv, beta * v, _NN)
    w_k = _bdot(inv, (beta * eg) * k, _NN)
    qk = _bdot(q, k, _NT) * decay
    return u_v, w_k, qk, q * eg, k * jnp.exp(gend - gcol), jnp.exp(gend)


def _dn_out(o, gn, z):
    return (_rms(o, gn) * _silu(z)).astype(BF16)


def _dn_prompt_body(alog_ref, dtb_ref, q_ref, k_ref, v_ref, ba_ref, z_ref, gn_ref, s0_ref, o_ref, s_ref,
                    kw_s, ku_s, ge_s, sp_s, *, chunk, hb, b_lane, a_lane):
    t = q_ref.shape[0]
    n = t // chunk
    d = DN_HEAD
    lane = lax.broadcasted_iota(jnp.int32, (1, LANES), 1)
    ba = ba_ref[...]
    r3 = lambda x: x.reshape(n, chunk, x.shape[-1])
    q_w, q_u = [], []
    for i in range(hb):
        h = pl.program_id(1) * hb + i
        cols = slice(i * d, (i + 1) * d)
        q = _l2n(q_ref[:, cols]) * (d ** -0.5)
        k = _l2n(k_ref[:, cols])
        b_raw = jnp.sum(jnp.where(lane == b_lane + h, ba, 0.0), axis=1, keepdims=True)
        a_raw = jnp.sum(jnp.where(lane == a_lane + h, ba, 0.0), axis=1, keepdims=True)
        beta = _sigmoid(b_raw)
        g = -jnp.exp(alog_ref[h]) * _softplus(a_raw + dtb_ref[h])
        u_v, w_k, qk, q_dec, k_dec, g_end = _dn_chunk_terms(r3(q), r3(k), r3(v_ref[:, cols]), r3(beta), r3(g))
        kw_s[i] = _bdot(k_dec, w_k, _TN).astype(BF16)
        ku_s[i] = _bdot(k_dec, u_v, _TN)
        ge_s[i] = jnp.broadcast_to(g_end, (n, 1, d))
        q_w.append(q_dec - _bdot(qk, w_k, _NN))
        q_u.append(_bdot(qk, u_v, _NN))

    def step(c, ss):
        out = []
        for i, s in enumerate(ss):
            sb = s.astype(BF16)
            sp_s[i, c] = sb
            out.append(ge_s[i, c] * s - _dot(kw_s[i, c], sb) + ku_s[i, c])
        return tuple(out)

    finals = lax.fori_loop(0, n, step, tuple(s0_ref[0, i] for i in range(hb)))
    for i in range(hb):
        s_ref[0, i] = finals[i]
        o = _bdot(q_w[i], sp_s[i], _NN) + q_u[i]
        cols = slice(i * d, (i + 1) * d)
        o_ref[:, cols] = _dn_out(o.reshape(t, d), gn_ref[...], z_ref[:, cols])


def deltanet_prompt(y, small, z, a_log, dt_bias, out_norm, s0, bsz, seq, nh, hb, ba_block, b_lane, a_lane):
    n = y.shape[0]
    chunk = min(DN_CHUNK, seq)
    d = DN_HEAD
    nc = seq // chunk
    assert nh % hb == 0
    smem = pl.BlockSpec(memory_space=pltpu.SMEM)
    col = lambda off: pl.BlockSpec((seq, hb * d), lambda b, h, off=off: (b, off // hb + h))
    return pl.pallas_call(
        functools.partial(_dn_prompt_body, chunk=chunk, hb=hb, b_lane=b_lane, a_lane=a_lane),
        grid=(bsz, nh // hb),
        in_specs=[smem, smem, col(0), col(nh), col(2 * nh),
                  pl.BlockSpec((seq, LANES), lambda b, h: (b, ba_block)),
                  pl.BlockSpec((seq, hb * d), lambda b, h: (b, h)),
                  pl.BlockSpec((1, d), lambda b, h: (0, 0)),
                  pl.BlockSpec((1, hb, d, d), lambda b, h: (b, h, 0, 0))],
        out_specs=[pl.BlockSpec((seq, hb * d), lambda b, h: (b, h)),
                   pl.BlockSpec((1, hb, d, d), lambda b, h: (b, h, 0, 0))],
        out_shape=[jax.ShapeDtypeStruct((n, nh * d), BF16), jax.ShapeDtypeStruct((bsz, nh, d, d), F32)],
        scratch_shapes=[pltpu.VMEM((hb, nc, d, d), BF16), pltpu.VMEM((hb, nc, d, d), F32),
                        pltpu.VMEM((hb, nc, 1, d), F32), pltpu.VMEM((hb, nc, d, d), BF16)],
        compiler_params=_cp("parallel", "parallel"),
        name="deltanet_prompt",
    )(a_log, dt_bias, y, y, y, small, z, out_norm.reshape(1, d), s0)


def _dn_sample_body(q_ref, k_ref, v_ref, b_ref, a_ref, z_ref, alog_ref, dtb_ref, gn_ref, s0_ref, o_ref, s_ref, *,
                    t_real):
    q = _l2n(q_ref[0]) * (DN_HEAD ** -0.5)
    k = _l2n(k_ref[0])
    v = v_ref[0]
    real = lax.broadcasted_iota(jnp.int32, (q.shape[1], 1), 0) < t_real
    beta = jnp.where(real, _sigmoid(b_ref[0][:, :, 0:1]), 0.0)
    g = jnp.where(real, -jnp.exp(alog_ref[...][:, :, 0:1]) * _softplus(a_ref[0][:, :, 0:1] + dtb_ref[...][:, :, 0:1]), 0.0)
    u_v, w_k, qk, q_dec, k_dec, g_end = _dn_chunk_terms(q, k, v, beta, g)
    s = s0_ref[0, 0]
    u = u_v - _bdot(w_k, s, _NN)
    o = _bdot(q_dec, s, _NN) + _bdot(qk, u, _NN)
    o_ref[0] = _dn_out(o, gn_ref[...], z_ref[0])
    s_ref[0] = g_end * s + _bdot(k_dec, u, _TN)


def deltanet_sample(q, k, v, b_raw, a_raw, z, a_log, dt_bias, out_norm, states, layer, t_real):
    bsz, nh, tp, d = q.shape
    blk = pl.BlockSpec((1, nh, tp, d), lambda b: (b, 0, 0, 0))
    par = pl.BlockSpec((nh, 1, d), lambda b: (0, 0, 0))
    st = pl.BlockSpec((1, nh, d, d), lambda b: (b, 0, 0, 0))
    st_in = pl.BlockSpec((1, 1, nh, d, d), lambda b: (layer, b, 0, 0, 0))
    bc = lambda p: jnp.broadcast_to(p.astype(F32).reshape(nh, 1, 1), (nh, 1, d))
    return pl.pallas_call(
        functools.partial(_dn_sample_body, t_real=t_real),
        grid=(bsz,),
        in_specs=[blk, blk, blk, blk, blk, blk, par, par, pl.BlockSpec((1, d), lambda b: (0, 0)), st_in],
        out_specs=[blk, st],
        out_shape=[jax.ShapeDtypeStruct((bsz, nh, tp, d), BF16), jax.ShapeDtypeStruct((bsz, nh, d, d), F32)],
        compiler_params=_cp("parallel"),
        name="deltanet_sample",
    )(q, k, v, b_raw, a_raw, z, bc(a_log), bc(dt_bias), out_norm.reshape(1, d), states)


def _merge_body(oa_ref, od_ref, om_ref, wa_ref, wd_ref, wm_ref, g0_ref, g1_ref, g2_ref, o_ref):
    acc = _sigmoid(g0_ref[...]) * _dot(oa_ref[...], wa_ref[...])
    acc = acc + _sigmoid(g1_ref[...]) * _dot(od_ref[...], wd_ref[...])
    acc = acc + _sigmoid(g2_ref[...]) * _dot(om_ref[...], wm_ref[...])
    o_ref[...] = acc.astype(o_ref.dtype)


def merge_branches(o_mla, o_dn, o_mem, w_mla, w_dn, w_mem, gate_raw, tm, tn, layer):
    n, d = o_mla.shape[0], w_mla.shape[1]
    nj = d // tn
    a_spec = lambda width: pl.BlockSpec((tm, width), lambda i, j: (i, 0))
    w_spec = lambda width: pl.BlockSpec((width, tn), lambda i, j: (layer, j))
    g_spec = lambda br: pl.BlockSpec((tm, tn), lambda i, j, br=br: (i, br * nj + j))
    return pl.pallas_call(
        _merge_body,
        grid=(n // tm, nj),
        in_specs=[a_spec(o_mla.shape[1]), a_spec(o_dn.shape[1]), a_spec(o_mem.shape[1]),
                  w_spec(o_mla.shape[1]), w_spec(o_dn.shape[1]), w_spec(o_mem.shape[1]),
                  g_spec(0), g_spec(1), g_spec(2)],
        out_specs=pl.BlockSpec((tm, tn), lambda i, j: (i, j)),
        out_shape=jax.ShapeDtypeStruct((n, d), BF16),
        compiler_params=_cp("parallel", "parallel"),
        name="merge_branches",
    )(o_mla, o_dn, o_mem, w_mla, w_dn, w_mem, gate_raw, gate_raw, gate_raw)


def _rope_tables(pos):
    half = QK_ROPE // 2
    inv_freq = ROPE_BASE ** (-jnp.arange(half, dtype=F32) / half)
    ang = pos.astype(F32)[:, None] * inv_freq[None, :]
    cos, sin = jnp.cos(ang), jnp.sin(ang)
    return jnp.concatenate([cos, cos], axis=1), jnp.concatenate([-sin, sin], axis=1)


PREF = dict(tm=1024, tn=1024, tn_small=512, tn_up=512, tn_down=512, norm_rows=256, mem_rows=512, flash_q=512,
            conv_rows=1024, conv_cols=512, ffn_cols=256, merge_m=512, merge_n=512, pages=32, pages_sub=32, dn_heads=2)


def _tile(n, pref, unit=1):
    if n <= pref:
        return n
    for t in range(pref - pref % unit, 0, -unit):
        if n % t == 0:
            return t
    return n


class _Weights:
    def __init__(self, w_in, w_uq, w_uk, w_uv, w_mem_kv, w_br_mla, w_br_dn, w_br_mem, w_out, w_up, w_down, dims):
        q_lora, kv_lora, dn_ch, nh_dn, dn_v, mem_d, d_model = dims
        depth = w_in.shape[0]
        o = 0
        cuts = {}
        for name, width in (("cq", q_lora), ("ckv", kv_lora), ("kpe", QK_ROPE), ("qkv", dn_ch), ("b", nh_dn),
                            ("a", nh_dn), ("z", dn_v), ("mq", mem_d), ("gate", 3 * d_model)):
            cuts[name] = (o, o + width)
            o += width
        rows = lambda w: w.astype(BF16).reshape(depth * w.shape[1], w.shape[2])
        c = lambda a, b: w_in[:, :, a:b]
        self.w_small = rows(jnp.concatenate([c(cuts["cq"][0], cuts["kpe"][1]), c(cuts["b"][0], cuts["a"][1])], axis=2))
        self.w_qkv = rows(c(*cuts["qkv"]))
        self.w_z = rows(c(*cuts["z"]))
        self.w_mq = rows(c(*cuts["mq"]))
        self.w_gate = rows(c(*cuts["gate"]))
        nh = w_uk.shape[1]
        heads = lambda w: w.astype(BF16).reshape((depth * nh,) + w.shape[2:])
        self.w_uq = heads(jnp.transpose(w_uq.reshape(depth, q_lora, nh, -1), (0, 2, 1, 3)))
        self.w_uk = heads(w_uk)
        self.w_uv = heads(w_uv)
        self.w_mem_kv = rows(w_mem_kv)
        self.w_br_mla = rows(w_br_mla)
        self.w_br_dn = rows(w_br_dn)
        self.w_br_mem = rows(w_br_mem)
        self.w_out = rows(w_out)
        self.w_up = rows(w_up)
        self.w_down = rows(w_down)


def _token_stage(x, wts, norm_mix, tm, l):
    h = rmsnorm(x, norm_mix, BF16, _tile(x.shape[0], PREF["norm_rows"], SUBLANES))
    mm = lambda w, tn: matmul(h, w, F32, tm, _tile(w.shape[1], tn, LANES), layer=l)
    tn, tns = PREF["tn"], PREF["tn_small"]
    return mm(wts.w_small, tns), mm(wts.w_qkv, tn), mm(wts.w_z, tn), mm(wts.w_mq, tns), mm(wts.w_gate, tn)


def _ffn_tail(x, merged, wts, norm_ffn, tm, l):
    x = matmul(merged, wts.w_out, F32, tm, _tile(wts.w_out.shape[1], PREF["tn"], LANES), res=x, layer=l)
    h = rmsnorm(x, norm_ffn, BF16, _tile(x.shape[0], PREF["norm_rows"], SUBLANES))
    u = matmul(h, wts.w_up, F32, tm, _tile(wts.w_up.shape[1], PREF["tn_up"], LANES), layer=l)
    return x, u


def _down(act, wts, x, tm, l):
    d_ff, d_model = act.shape[1], wts.w_down.shape[1]
    return matmul(act, wts.w_down, F32, tm, _tile(d_model, PREF["tn_down"], LANES), tk=_tile(d_ff, d_ff // 2, LANES),
                  res=x, layer=l)


def kernel(x_prompt, x_sample, mem_prompt, cache_ckv, cache_kpe, page_table, cache_mem_k, cache_mem_v, state_dn,
           state_dn_conv, state_ffn_conv, norm_mix, w_in, q_norm, kv_norm, kpe_norm, qh_norm, w_uq, w_uk, w_uv,
           dn_conv_w, dn_a_log, dn_dt_bias, dn_out_norm, mem_norm, w_mem_kv, mk_norm, mq_norm, w_br_mla, w_br_dn,
           w_br_mem, w_out, norm_ffn, w_up, ffn_conv_w, w_down):
    bp, tp, d_model = x_prompt.shape
    bs, ts, _ = x_sample.shape
    depth = w_in.shape[0]
    nh = w_uk.shape[1]
    q_lora = w_uq.shape[1]
    kv_lora = w_uk.shape[3]
    nh_dn = dn_a_log.shape[1]
    dn_v = w_br_dn.shape[1]
    dn_ch = dn_conv_w.shape[2]
    dn_qk = (dn_ch - dn_v) // 2
    mem_tok = mem_prompt.shape[1]
    mem_d = w_br_mem.shape[1]
    d_ff = w_down.shape[1]
    past_len = page_table.shape[1] * cache_ckv.shape[2]
    scale = (QK_NOPE + QK_ROPE) ** -0.5
    dims = (q_lora, kv_lora, dn_ch, nh_dn, dn_v, mem_d, d_model)
    ba_off = q_lora + kv_lora + QK_ROPE
    ba_block, b_lane = divmod(ba_off, LANES)
    a_lane = b_lane + nh_dn
    assert a_lane + nh_dn <= LANES and dn_qk == nh_dn * DN_HEAD

    cos_p, sin_p = _rope_tables(jnp.arange(tp, dtype=jnp.int32))
    cos_s, sin_s = _rope_tables(past_len + jnp.arange(ts, dtype=jnp.int32))
    cos_s, sin_s = jnp.tile(cos_s, (bs, 1)), jnp.tile(sin_s, (bs, 1))

    n_p, n_s = bp * tp, bs * ts
    tm_p = _tile(n_p, PREF["tm"], SUBLANES)
    tm_s = _tile(n_s, PREF["tm"], SUBLANES)
    conv_tm = _tile(tp, PREF["conv_rows"], SUBLANES)
    mem_tm = _tile(bp * mem_tok, PREF["mem_rows"], SUBLANES)
    tp_pad = SUBLANES * -(-ts // SUBLANES)
    xp = x_prompt.reshape(n_p, d_model)
    xs = x_sample.reshape(n_s, d_model)
    mem_rows = mem_prompt.reshape(bp * mem_tok, d_model)
    pages_per_step = _tile(page_table.shape[1], PREF["pages"])
    cache_kpe_t = jnp.swapaxes(cache_kpe, 2, 3)
    zeros_state = jnp.zeros((bp, nh_dn, DN_HEAD, DN_HEAD), F32)
    zeros_dn_hist = jnp.zeros((bp, dn_conv_w.shape[1] - 1, dn_ch), F32)
    zeros_ffn_hist = jnp.zeros((bp, ffn_conv_w.shape[1] - 1, 2 * d_ff), F32)

    outs_p = {k: [] for k in ("ckv", "kpe", "dns", "dnh", "ffh", "mk", "mv")}
    outs_s = {k: [] for k in ("ckv", "kpe", "dns", "dnh", "ffh")}

    wts = _Weights(w_in, w_uq, w_uk, w_uv, w_mem_kv, w_br_mla, w_br_dn, w_br_mem, w_out, w_up, w_down, dims)
    for l in range(depth):
        dn_w = dn_conv_w[l]
        ffn_w = ffn_conv_w[l]
        dn_wd = dn_w.shape[0]
        ffn_wd = ffn_w.shape[0]

        mem_h = rmsnorm(mem_rows, mem_norm[l], BF16, _tile(bp * mem_tok, PREF["norm_rows"], SUBLANES))
        mem_kv = matmul(mem_h, wts.w_mem_kv, F32, mem_tm, _tile(2 * mem_d, PREF["tn_small"], LANES), layer=l)
        mk_p = headnorm(mem_kv[:, :mem_d], mk_norm[l], MEM_HD, mem_tm)
        mv_p = mem_kv[:, mem_d:]

        small, qkv, z, mq, gate_raw = _token_stage(xp, wts, norm_mix[l], tm_p, l)
        ql, qr, ckv, ckvb, kpe, kpeb = mla_prep(small, q_norm[l], kv_norm[l], kpe_norm[l], qh_norm[l], wts.w_uq,
                                                wts.w_uk, cos_p, sin_p, _tile(tp, PREF["tm"], SUBLANES), scale, nh, l)
        o_lat = flash_prompt(ql, qr, ckvb, kpeb, bp, tp, _tile(tp, PREF["flash_q"], SUBLANES), 1)
        o_mla = uv_project(o_lat, wts.w_uv, tm_p, l)
        y = conv_silu_rows(qkv, zeros_dn_hist, dn_w, tp, conv_tm, _tile(dn_ch, PREF["conv_cols"], LANES))
        o_dn, dns_p = deltanet_prompt(y, small, z, dn_a_log[l], dn_dt_bias[l], dn_out_norm[l], zeros_state,
                                      bp, tp, nh_dn, _tile(nh_dn, PREF["dn_heads"]), ba_block, b_lane, a_lane)
        o_mem = mem_attend(mq.reshape(bp, tp, mem_d), mk_p.reshape(bp, mem_tok, mem_d),
                           mv_p.reshape(bp, mem_tok, mem_d), mq_norm[l], _tile(tp, PREF["flash_q"], SUBLANES)).reshape(n_p, mem_d)
        merged = merge_branches(o_mla, o_dn, o_mem, wts.w_br_mla, wts.w_br_dn, wts.w_br_mem, gate_raw,
                                _tile(n_p, PREF["merge_m"], SUBLANES), _tile(d_model, PREF["merge_n"], LANES), l)
        xp, u = _ffn_tail(xp, merged, wts, norm_ffn[l], tm_p, l)
        act = conv_swiglu_rows(u, zeros_ffn_hist, ffn_w, tp, conv_tm, _tile(d_ff, PREF["ffn_cols"], LANES))
        xp = _down(act, wts, xp, tm_p, l)

        outs_p["ckv"].append(ckv.reshape(bp, tp, kv_lora))
        outs_p["kpe"].append(kpe.reshape(bp, tp, QK_ROPE))
        outs_p["dns"].append(dns_p)
        outs_p["dnh"].append(qkv.reshape(bp, tp, dn_ch)[:, tp - (dn_wd - 1):])
        outs_p["ffh"].append(u.reshape(bp, tp, 2 * d_ff)[:, tp - (ffn_wd - 1):])
        outs_p["mk"].append(mk_p.reshape(bp, mem_tok, mem_d // MEM_HD, MEM_HD))
        outs_p["mv"].append(mv_p.reshape(bp, mem_tok, mem_d // MEM_HD, MEM_HD))

        small, qkv, z, mq, gate_raw = _token_stage(xs, wts, norm_mix[l], tm_s, l)
        ql, qr, ckv, ckvb, kpe, kpeb = mla_prep(small, q_norm[l], kv_norm[l], kpe_norm[l], qh_norm[l], wts.w_uq,
                                                wts.w_uk, cos_s, sin_s, tm_s, scale, nh, l)
        to_bht = lambda a: jnp.transpose(a.reshape(nh, bs, ts, -1), (1, 0, 2, 3)).reshape(bs, nh * ts, -1)
        o_lat = paged_sample(to_bht(ql), to_bht(qr), ckv.reshape(bs, ts, kv_lora), kpe.reshape(bs, ts, QK_ROPE),
                             cache_ckv, cache_kpe_t, page_table, l, pages_per_step,
                             _tile(pages_per_step, PREF["pages_sub"]))
        o_lat = jnp.transpose(o_lat.reshape(bs, nh, ts, kv_lora), (1, 0, 2, 3)).reshape(nh, n_s, kv_lora)
        o_mla = uv_project(o_lat, wts.w_uv, tm_s, l)

        hist = state_dn_conv[l]
        qkv_tm = jnp.transpose(qkv.reshape(bs, ts, dn_ch), (1, 0, 2))
        y_tm = conv_silu_tm(qkv_tm, jnp.transpose(hist, (1, 0, 2)), dn_w, _tile(dn_ch, PREF["conv_cols"], LANES))
        pad_t = lambda a: jnp.pad(a, ((0, 0), (0, 0), (0, tp_pad - ts), (0, 0)))
        y5 = pad_t(jnp.transpose(y_tm.reshape(ts, bs, 3 * nh_dn, DN_HEAD), (1, 2, 0, 3)))
        z4 = pad_t(jnp.transpose(z.reshape(bs, ts, nh_dn, DN_HEAD), (0, 2, 1, 3)))
        ba = jnp.transpose(small[:, ba_off:ba_off + 2 * nh_dn].reshape(bs, ts, 2 * nh_dn), (0, 2, 1))[..., None]
        ba = pad_t(jnp.broadcast_to(ba, (bs, 2 * nh_dn, ts, DN_HEAD)))
        o_dn, dns_s = deltanet_sample(y5[:, :nh_dn], y5[:, nh_dn:2 * nh_dn], y5[:, 2 * nh_dn:], ba[:, :nh_dn],
                                      ba[:, nh_dn:], z4, dn_a_log[l], dn_dt_bias[l], dn_out_norm[l], state_dn, l, ts)
        o_dn = jnp.transpose(o_dn[:, :, :ts], (0, 2, 1, 3)).reshape(n_s, dn_v)
        dnh_s = jnp.concatenate([hist, qkv.reshape(bs, ts, dn_ch)], axis=1)[:, ts:]

        mq_pad = jnp.pad(mq.reshape(bs, ts, mem_d), ((0, 0), (0, tp_pad - ts), (0, 0)))
        o_mem = mem_attend(mq_pad, cache_mem_k.reshape(depth * bs, mem_tok, mem_d),
                           cache_mem_v.reshape(depth * bs, mem_tok, mem_d), mq_norm[l], tp_pad, kv_first=l * bs)
        o_mem = o_mem[:, :ts].reshape(n_s, mem_d)
        merged = merge_branches(o_mla, o_dn, o_mem, wts.w_br_mla, wts.w_br_dn, wts.w_br_mem, gate_raw,
                                _tile(n_s, PREF["merge_m"], SUBLANES), _tile(d_model, PREF["merge_n"], LANES), l)
        xs, u = _ffn_tail(xs, merged, wts, norm_ffn[l], tm_s, l)
        fh = state_ffn_conv[l]
        u3 = u.reshape(bs, ts, 2 * d_ff)
        act = conv_swiglu_tm(jnp.transpose(u3, (1, 0, 2)), jnp.transpose(fh, (1, 0, 2)), ffn_w,
                             _tile(d_ff, PREF["ffn_cols"], LANES))
        act = jnp.transpose(act, (1, 0, 2)).reshape(n_s, d_ff)
        xs = _down(act, wts, xs, tm_s, l)

        outs_s["ckv"].append(ckv.reshape(bs, ts, kv_lora))
        outs_s["kpe"].append(kpe.reshape(bs, ts, QK_ROPE))
        outs_s["dns"].append(dns_s)
        outs_s["dnh"].append(dnh_s)
        outs_s["ffh"].append(jnp.concatenate([fh, u3], axis=1)[:, ts:])

    st = lambda d, k: jnp.stack(d[k])
    return (xp.reshape(bp, tp, d_model), xs.reshape(bs, ts, d_model),
            st(outs_p, "ckv"), st(outs_p, "kpe"), st(outs_p, "dns"), st(outs_p, "dnh"), st(outs_p, "ffh"),
            st(outs_p, "mk"), st(outs_p, "mv"),
            st(outs_s, "ckv"), st(outs_s, "kpe"), st(outs_s, "dns"), st(outs_s, "dnh"), st(outs_s, "ffh"))
```

```python
import functools
import math

import jax
import jax.numpy as jnp
from jax import lax
from jax.experimental import pallas as pl
from jax.experimental.pallas import tpu as pltpu

F32 = jnp.float32
BF16 = jnp.bfloat16
EPS = 1e-6
ROPE_BASE = 10000.0
LANES = 128
SUBLANES = 8
VMEM_LIMIT_BYTES = 56 * 1024 * 1024

DN_CHUNK = 64
DN_HEAD = 128
MEM_HD = 128
QK_NOPE = 128
QK_ROPE = 64


def _cp(*sem):
    return pltpu.CompilerParams(dimension_semantics=sem, vmem_limit_bytes=VMEM_LIMIT_BYTES)


def _rms(x, g):
    return x * lax.rsqrt(jnp.mean(x * x, axis=-1, keepdims=True) + EPS) * g


def _silu(x):
    return x * (1.0 / (1.0 + jnp.exp(-x)))


def _sigmoid(x):
    return 1.0 / (1.0 + jnp.exp(-x))


def _softplus(x):
    return jnp.maximum(x, 0.0) + jnp.log(1.0 + jnp.exp(-jnp.abs(x)))


def _dot(a, b):
    return jnp.dot(a, b, preferred_element_type=F32)


def _dot_nt(a, b):
    return lax.dot_general(a, b, (((1,), (1,)), ((), ())), preferred_element_type=F32)


def _rmsnorm_body(x_ref, g_ref, o_ref):
    o_ref[...] = _rms(x_ref[...].astype(F32), g_ref[...]).astype(o_ref.dtype)


def rmsnorm(x, g, out_dtype, tm):
    n, d = x.shape
    return pl.pallas_call(
        _rmsnorm_body,
        grid=(n // tm,),
        in_specs=[pl.BlockSpec((tm, d), lambda i: (i, 0)), pl.BlockSpec((1, d), lambda i: (0, 0))],
        out_specs=pl.BlockSpec((tm, d), lambda i: (i, 0)),
        out_shape=jax.ShapeDtypeStruct((n, d), out_dtype),
        compiler_params=_cp("parallel"),
        name="rmsnorm",
    )(x, g.reshape(1, d).astype(F32))


def _headnorm_body(x_ref, g_ref, o_ref, *, hd):
    x = x_ref[...]
    for h in range(x.shape[1] // hd):
        o_ref[:, h * hd:(h + 1) * hd] = _rms(x[:, h * hd:(h + 1) * hd], g_ref[...])


def headnorm(x, g, hd, tm):
    n, d = x.shape
    return pl.pallas_call(
        functools.partial(_headnorm_body, hd=hd),
        grid=(n // tm,),
        in_specs=[pl.BlockSpec((tm, d), lambda i: (i, 0)), pl.BlockSpec((1, hd), lambda i: (0, 0))],
        out_specs=pl.BlockSpec((tm, d), lambda i: (i, 0)),
        out_shape=jax.ShapeDtypeStruct((n, d), F32),
        compiler_params=_cp("parallel"),
        name="headnorm",
    )(x, g.reshape(1, hd).astype(F32))


def _mm_body(*refs, nk, has_res):
    if has_res:
        a_ref, b_ref, r_ref, o_ref = refs[:4]
    else:
        a_ref, b_ref, o_ref = refs[:3]
        r_ref = None
    if nk == 1:
        acc = _dot(a_ref[...], b_ref[...])
        if has_res:
            acc = r_ref[...] + acc
        o_ref[...] = acc.astype(o_ref.dtype)
        return
    acc_ref = refs[-1]
    k = pl.program_id(2)

    @pl.when(k == 0)
    def _():
        acc_ref[...] = jnp.zeros_like(acc_ref)

    acc_ref[...] += _dot(a_ref[...], b_ref[...])

    @pl.when(k == nk - 1)
    def _():
        acc = acc_ref[...]
        if has_res:
            acc = r_ref[...] + acc
        o_ref[...] = acc.astype(o_ref.dtype)


def matmul(a, b, out_dtype, tm, tn, tk=None, res=None, layer=0):
    m, kk = a.shape
    n = b.shape[1]
    tk = kk if tk is None else tk
    nk = kk // tk
    assert m % tm == 0 and n % tn == 0 and kk % tk == 0 and b.shape[0] % kk == 0
    in_specs = [pl.BlockSpec((tm, tk), lambda i, j, k: (i, k)),
                pl.BlockSpec((tk, tn), lambda i, j, k: (layer * nk + k, j))]
    args = [a, b]
    if res is not None:
        in_specs.append(pl.BlockSpec((tm, tn), lambda i, j, k: (i, j)))
        args.append(res)
    return pl.pallas_call(
        functools.partial(_mm_body, nk=nk, has_res=res is not None),
        grid=(m // tm, n // tn, nk),
        in_specs=in_specs,
        out_specs=pl.BlockSpec((tm, tn), lambda i, j, k: (i, j)),
        out_shape=jax.ShapeDtypeStruct((m, n), out_dtype),
        scratch_shapes=[pltpu.VMEM((tm, tn), F32)] if nk > 1 else [],
        compiler_params=_cp("parallel", "parallel", "arbitrary"),
        name="matmul",
    )(*args)


def _rope(x, cos2, sin2):
    half = x.shape[1] // 2
    rot = jnp.concatenate([x[:, half:], x[:, :half]], axis=1)
    return x * cos2 + rot * sin2


def _mla_prep_body(small_ref, qn_ref, kvn_ref, kpn_ref, qhn_ref, wuq_ref, wuk_ref, cos_ref, sin_ref,
                   ql_ref, qr_ref, ckv_ref, ckvb_ref, kpe_ref, kpeb_ref, cqn_ref, *, q_lora, kv_lora, scale):
    h = pl.program_id(1)

    @pl.when(h == 0)
    def _():
        cqn_ref[...] = _rms(small_ref[:, 0:q_lora], qn_ref[...]).astype(BF16)
        ckv = _rms(small_ref[:, q_lora:q_lora + kv_lora], kvn_ref[...])
        ckv_ref[...] = ckv
        ckvb_ref[...] = ckv.astype(BF16)
        o = q_lora + kv_lora
        kpe = _rope(_rms(small_ref[:, o:o + QK_ROPE], kpn_ref[...]), cos_ref[...], sin_ref[...])
        kpe_ref[...] = kpe
        kpeb_ref[...] = kpe.astype(BF16)

    q = _rms(_dot(cqn_ref[...], wuq_ref[0]), qhn_ref[...])
    ql = _dot(q[:, :QK_NOPE].astype(BF16), wuk_ref[0]) * scale
    ql_ref[0] = ql.astype(BF16)
    qr_ref[0] = (_rope(q[:, QK_NOPE:], cos_ref[...], sin_ref[...]) * scale).astype(BF16)


def mla_prep(small, q_norm, kv_norm, kpe_norm, qh_norm, wuq_h, wuk, cos2, sin2, tm, scale, nh, layer):
    n = small.shape[0]
    _, q_lora, hq = wuq_h.shape
    kv_lora = wuk.shape[2]
    h0 = layer * nh
    nrep = cos2.shape[0] // tm
    row = lambda i, h: (i, 0)
    const = lambda i, h: (0, 0)
    tab = lambda i, h: (i % nrep, 0)
    return pl.pallas_call(
        functools.partial(_mla_prep_body, q_lora=q_lora, kv_lora=kv_lora, scale=scale),
        grid=(n // tm, nh),
        in_specs=[
            pl.BlockSpec((tm, small.shape[1]), row),
            pl.BlockSpec((1, q_lora), const), pl.BlockSpec((1, kv_lora), const),
            pl.BlockSpec((1, QK_ROPE), const), pl.BlockSpec((1, hq), const),
            pl.BlockSpec((1, q_lora, hq), lambda i, h: (h0 + h, 0, 0)),
            pl.BlockSpec((1, QK_NOPE, kv_lora), lambda i, h: (h0 + h, 0, 0)),
            pl.BlockSpec((tm, QK_ROPE), tab), pl.BlockSpec((tm, QK_ROPE), tab),
        ],
        out_specs=[
            pl.BlockSpec((1, tm, kv_lora), lambda i, h: (h, i, 0)),
            pl.BlockSpec((1, tm, QK_ROPE), lambda i, h: (h, i, 0)),
            pl.BlockSpec((tm, kv_lora), row), pl.BlockSpec((tm, kv_lora), row),
            pl.BlockSpec((tm, QK_ROPE), row), pl.BlockSpec((tm, QK_ROPE), row),
        ],
        out_shape=[
            jax.ShapeDtypeStruct((nh, n, kv_lora), BF16), jax.ShapeDtypeStruct((nh, n, QK_ROPE), BF16),
            jax.ShapeDtypeStruct((n, kv_lora), F32), jax.ShapeDtypeStruct((n, kv_lora), BF16),
            jax.ShapeDtypeStruct((n, QK_ROPE), F32), jax.ShapeDtypeStruct((n, QK_ROPE), BF16),
        ],
        scratch_shapes=[pltpu.VMEM((tm, q_lora), BF16)],
        compiler_params=_cp("parallel", "arbitrary"),
        name="mla_prep",
    )(small, q_norm.reshape(1, -1), kv_norm.reshape(1, -1), kpe_norm.reshape(1, -1), qh_norm.reshape(1, -1),
      wuq_h, wuk, cos2, sin2)


def _flash_body(ql_ref, qr_ref, ckv_ref, kpe_ref, wuv_ref, o_ref, m_ref, l_ref, acc_ref, *, tq):
    qi = pl.program_id(1)
    m_ref[...] = jnp.full_like(m_ref, -jnp.inf)
    l_ref[...] = jnp.zeros_like(l_ref)
    acc_ref[...] = jnp.zeros_like(acc_ref)

    def scores(j):
        ks = pl.multiple_of(j * tq, tq)
        return _dot_nt(ql_ref[0], ckv_ref[pl.ds(ks, tq), :]) + _dot_nt(qr_ref[0], kpe_ref[pl.ds(ks, tq), :])

    def fold(j, s, masked):
        kc = ckv_ref[pl.ds(pl.multiple_of(j * tq, tq), tq), :]
        if masked:
            r = lax.broadcasted_iota(jnp.int32, (tq, tq), 0)
            c = lax.broadcasted_iota(jnp.int32, (tq, tq), 1)
            s = jnp.where(c <= r, s, -jnp.inf)
        m_old = m_ref[...]
        m_new = jnp.maximum(m_old, jnp.max(s, axis=1, keepdims=True))
        p = jnp.exp(s - m_new)
        corr = jnp.exp(m_old - m_new)
        l_ref[...] = l_ref[...] * corr + jnp.sum(p, axis=1, keepdims=True)
        acc_ref[...] = acc_ref[...] * corr + _dot(p.astype(BF16), kc)
        m_ref[...] = m_new

    def body(j, s):
        nxt = scores(j + 1)
        fold(j, s, False)
        return nxt

    fold(qi, lax.fori_loop(0, qi, body, scores(0)), True)
    o_lat = (acc_ref[...] / l_ref[...]).astype(BF16)
    o_ref[...] = _dot(o_lat, wuv_ref[0]).astype(o_ref.dtype)


def flash_prompt(ql, qr, ckvb, kpeb, wuv, bsz, seq, tq, layer):
    nh, n, kv_lora = ql.shape
    vh = wuv.shape[2]
    nq = seq // tq
    h0 = layer * nh
    qmap = lambda b, qi, h: (h, b * nq + qi, 0)
    kmap = lambda b, qi, h: (b, 0)
    return pl.pallas_call(
        functools.partial(_flash_body, tq=tq),
        grid=(bsz, nq, nh),
        in_specs=[
            pl.BlockSpec((1, tq, kv_lora), qmap), pl.BlockSpec((1, tq, QK_ROPE), qmap),
            pl.BlockSpec((seq, kv_lora), kmap), pl.BlockSpec((seq, QK_ROPE), kmap),
            pl.BlockSpec((1, kv_lora, vh), lambda b, qi, h: (h0 + h, 0, 0)),
        ],
        out_specs=pl.BlockSpec((tq, vh), lambda b, qi, h: (b * nq + qi, h)),
        out_shape=jax.ShapeDtypeStruct((n, nh * vh), BF16),
        scratch_shapes=[pltpu.VMEM((tq, 1), F32), pltpu.VMEM((tq, 1), F32), pltpu.VMEM((tq, kv_lora), F32)],
        compiler_params=_cp("parallel", "parallel", "arbitrary"),
        name="flash_prompt",
    )(ql, qr, ckvb, kpeb, wuv)


def _paged_body(pt_ref, ql_ref, qr_ref, cn_ref, pn_ref, *refs, pg, sub, t_new):
    ck_refs = refs[:pg]
    kp_refs = refs[pg:2 * pg]
    o_ref, m_ref, l_ref, acc_ref = refs[2 * pg:]
    p = pl.program_id(1)
    ql = ql_ref[0]
    qr = qr_ref[0]

    @pl.when(p == 0)
    def _():
        cn = cn_ref[0].astype(BF16)
        s = _dot_nt(ql, cn) + _dot_nt(qr, pn_ref[0].astype(BF16))
        r = lax.rem(lax.broadcasted_iota(jnp.int32, s.shape, 0), t_new)
        c = lax.broadcasted_iota(jnp.int32, s.shape, 1)
        s = jnp.where(c <= r, s, -jnp.inf)
        m = jnp.max(s, axis=1, keepdims=True)
        pr = jnp.exp(s - m)
        m_ref[...] = m
        l_ref[...] = jnp.sum(pr, axis=1, keepdims=True)
        acc_ref[...] = _dot(pr.astype(BF16), cn)

    m_run = m_ref[...]
    l_run = l_ref[...]
    acc = acc_ref[...]
    for g0 in range(0, pg, sub):
        js = range(g0, g0 + sub)
        kcs = [ck_refs[j][0, 0].astype(BF16) for j in js]
        ss = [_dot_nt(ql, kc) + _dot(qr, kp_refs[j][0, 0].astype(BF16)) for j, kc in zip(js, kcs)]
        m_new = m_run
        for s in ss:
            m_new = jnp.maximum(m_new, jnp.max(s, axis=1, keepdims=True))
        corr = jnp.exp(m_run - m_new)
        l_run = l_run * corr
        acc = acc * corr
        for s, kc in zip(ss, kcs):
            pr = jnp.exp(s - m_new)
            l_run = l_run + jnp.sum(pr, axis=1, keepdims=True)
            acc = acc + _dot(pr.astype(BF16), kc)
        m_run = m_new
    m_ref[...] = m_run
    l_ref[...] = l_run
    acc_ref[...] = acc

    @pl.when(p == pl.num_programs(1) - 1)
    def _():
        o_ref[0] = (acc_ref[...] / l_ref[...]).astype(o_ref.dtype)


def paged_sample(ql, qr, ckv_new, kpe_new, cache_ckv, cache_kpe_t, page_table, layer, pg, sub):
    bsz, rows, kv_lora = ql.shape
    t_new = ckv_new.shape[1]
    n_pages = page_table.shape[1]
    page = cache_ckv.shape[2]
    assert n_pages % pg == 0 and pg % sub == 0
    fixed = lambda b, p, pt: (b, 0, 0)

    def page_map(j):
        return lambda b, p, pt: (layer, pt[b, p * pg + j], 0, 0)

    in_specs = [
        pl.BlockSpec((1, rows, kv_lora), fixed), pl.BlockSpec((1, rows, QK_ROPE), fixed),
        pl.BlockSpec((1, t_new, kv_lora), fixed), pl.BlockSpec((1, t_new, QK_ROPE), fixed),
    ]
    in_specs += [pl.BlockSpec((1, 1, page, kv_lora), page_map(j)) for j in range(pg)]
    in_specs += [pl.BlockSpec((1, 1, QK_ROPE, page), page_map(j)) for j in range(pg)]
    return pl.pallas_call(
        functools.partial(_paged_body, pg=pg, sub=sub, t_new=t_new),
        grid_spec=pltpu.PrefetchScalarGridSpec(
            num_scalar_prefetch=1,
            grid=(bsz, n_pages // pg),
            in_specs=in_specs,
            out_specs=pl.BlockSpec((1, rows, kv_lora), fixed),
            scratch_shapes=[pltpu.VMEM((rows, 1), F32), pltpu.VMEM((rows, 1), F32), pltpu.VMEM((rows, kv_lora), F32)],
        ),
        out_shape=jax.ShapeDtypeStruct((bsz, rows, kv_lora), BF16),
        compiler_params=_cp("parallel", "arbitrary"),
        name="paged_sample",
    )(page_table, ql, qr, ckv_new, kpe_new, *([cache_ckv] * pg), *([cache_kpe_t] * pg))


def _uv_body(o_ref, w_ref, out_ref):
    out_ref[...] = _dot(o_ref[0], w_ref[0]).astype(out_ref.dtype)


def uv_project(o_lat, wuv, tm, layer):
    nh, n, kv_lora = o_lat.shape
    vh = wuv.shape[2]
    h0 = layer * nh
    return pl.pallas_call(
        _uv_body,
        grid=(n // tm, nh),
        in_specs=[pl.BlockSpec((1, tm, kv_lora), lambda i, h: (h, i, 0)),
                  pl.BlockSpec((1, kv_lora, vh), lambda i, h: (h0 + h, 0, 0))],
        out_specs=pl.BlockSpec((tm, vh), lambda i, h: (i, h)),
        out_shape=jax.ShapeDtypeStruct((n, nh * vh), BF16),
        compiler_params=_cp("parallel", "parallel"),
        name="uv_project",
    )(o_lat, wuv)


def _mem_attn_body(q_ref, k_ref, v_ref, g_ref, o_ref, *, nh):
    q = q_ref[0]
    for h in range(nh):
        sl = slice(h * MEM_HD, (h + 1) * MEM_HD)
        qh = _rms(q[:, sl], g_ref[...]).astype(BF16)
        s = _dot_nt(qh, k_ref[0, :, sl].astype(BF16)) * (MEM_HD ** -0.5)
        m = jnp.max(s, axis=1, keepdims=True)
        p = jnp.exp(s - m)
        l = jnp.sum(p, axis=1, keepdims=True)
        o = _dot(p.astype(BF16), v_ref[0, :, sl].astype(BF16)) / l
        o_ref[0, :, sl] = o.astype(o_ref.dtype)


def mem_attend(q, k, v, mq_norm, tq, kv_first=0):
    bsz, t, d = q.shape
    m = k.shape[1]
    return pl.pallas_call(
        functools.partial(_mem_attn_body, nh=d // MEM_HD),
        grid=(bsz, t // tq),
        in_specs=[pl.BlockSpec((1, tq, d), lambda b, i: (b, i, 0)),
                  pl.BlockSpec((1, m, d), lambda b, i: (kv_first + b, 0, 0)),
                  pl.BlockSpec((1, m, d), lambda b, i: (kv_first + b, 0, 0)),
                  pl.BlockSpec((1, MEM_HD), lambda b, i: (0, 0))],
        out_specs=pl.BlockSpec((1, tq, d), lambda b, i: (b, i, 0)),
        out_shape=jax.ShapeDtypeStruct((bsz, t, d), BF16),
        compiler_params=_cp("parallel", "parallel"),
        name="mem_attend",
    )(q, k, v, mq_norm.reshape(1, MEM_HD))


def _conv_rows(x_ref, prev_ref, hist_ref, w_ref, xs_ref, is_start, width):
    tm = x_ref.shape[0]
    xs_ref[0:SUBLANES, :] = jnp.where(is_start, hist_ref[0], prev_ref[...])
    xs_ref[SUBLANES:, :] = x_ref[...]
    y = xs_ref[pl.ds(SUBLANES, tm), :] * w_ref[width - 1:width, :]
    for i in range(width - 1):
        y = y + xs_ref[pl.ds(SUBLANES - (width - 1) + i, tm), :] * w_ref[i:i + 1, :]
    return y


def _conv_swiglu_body(g_ref, gp_ref, gh_ref, gw_ref, u_ref, up_ref, uh_ref, uw_ref, o_ref, gs_ref, us_ref, *,
                      width, tiles_per_seq):
    is_start = pl.program_id(1) % tiles_per_seq == 0
    gate = _conv_rows(g_ref, gp_ref, gh_ref, gw_ref, gs_ref, is_start, width)
    up = _conv_rows(u_ref, up_ref, uh_ref, uw_ref, us_ref, is_start, width)
    o_ref[...] = (_silu(gate) * up).astype(o_ref.dtype)


def _hist8(hist):
    return jnp.pad(hist, ((0, 0), (SUBLANES - hist.shape[1], 0), (0, 0)))


def conv_swiglu_rows(u, hist, w, seq, tm, tc):
    n, c2 = u.shape
    f = c2 // 2
    width = w.shape[0]
    tps = seq // tm
    r8 = tm // SUBLANES
    nf = f // tc
    h8 = _hist8(hist)

    def specs(off):
        return [pl.BlockSpec((tm, tc), lambda j, i: (i, j + off)),
                pl.BlockSpec((SUBLANES, tc), lambda j, i: (jnp.maximum(i * r8 - 1, 0), j + off)),
                pl.BlockSpec((1, SUBLANES, tc), lambda j, i: (i // tps, 0, j + off)),
                pl.BlockSpec((width, tc), lambda j, i: (0, j + off))]

    return pl.pallas_call(
        functools.partial(_conv_swiglu_body, width=width, tiles_per_seq=tps),
        grid=(nf, n // tm),
        in_specs=specs(0) + specs(nf),
        out_specs=pl.BlockSpec((tm, tc), lambda j, i: (i, j)),
        out_shape=jax.ShapeDtypeStruct((n, f), BF16),
        scratch_shapes=[pltpu.VMEM((tm + SUBLANES, tc), F32), pltpu.VMEM((tm + SUBLANES, tc), F32)],
        compiler_params=_cp("parallel", "parallel"),
        name="conv_swiglu_rows",
    )(u, u, h8, w, u, u, h8, w)


def _conv_tm(x_ref, h_ref, w_ref, width):
    t = x_ref.shape[0]
    rows = [h_ref[i] for i in range(width - 1)] + [x_ref[i] for i in range(t)]
    out = []
    for s in range(t):
        y = rows[s] * w_ref[0:1, :]
        for i in range(1, width):
            y = y + rows[s + i] * w_ref[i:i + 1, :]
        out.append(y)
    return out


def _conv_silu_tm_body(x_ref, h_ref, w_ref, o_ref, *, width):
    for s, y in enumerate(_conv_tm(x_ref, h_ref, w_ref, width)):
        o_ref[s] = _silu(y)


def _conv_swiglu_tm_body(g_ref, gh_ref, gw_ref, u_ref, uh_ref, uw_ref, o_ref, *, width):
    gate = _conv_tm(g_ref, gh_ref, gw_ref, width)
    up = _conv_tm(u_ref, uh_ref, uw_ref, width)
    for s in range(len(gate)):
        o_ref[s] = (_silu(gate[s]) * up[s]).astype(o_ref.dtype)


def conv_silu_tm(x, hist, w, tc):
    t, b, c = x.shape
    width = w.shape[0]
    return pl.pallas_call(
        functools.partial(_conv_silu_tm_body, width=width),
        grid=(c // tc,),
        in_specs=[pl.BlockSpec((t, b, tc), lambda j: (0, 0, j)),
                  pl.BlockSpec((width - 1, b, tc), lambda j: (0, 0, j)),
                  pl.BlockSpec((width, tc), lambda j: (0, j))],
        out_specs=pl.BlockSpec((t, b, tc), lambda j: (0, 0, j)),
        out_shape=jax.ShapeDtypeStruct((t, b, c), F32),
        compiler_params=_cp("parallel"),
        name="conv_silu_tm",
    )(x, hist, w)


def conv_swiglu_tm(u, hist, w, tc):
    t, b, c2 = u.shape
    f = c2 // 2
    nf = f // tc
    width = w.shape[0]

    def specs(off):
        return [pl.BlockSpec((t, b, tc), lambda j: (0, 0, j + off)),
                pl.BlockSpec((width - 1, b, tc), lambda j: (0, 0, j + off)),
                pl.BlockSpec((width, tc), lambda j: (0, j + off))]

    return pl.pallas_call(
        functools.partial(_conv_swiglu_tm_body, width=width),
        grid=(nf,),
        in_specs=specs(0) + specs(nf),
        out_specs=pl.BlockSpec((t, b, tc), lambda j: (0, 0, j)),
        out_shape=jax.ShapeDtypeStruct((t, b, f), BF16),
        compiler_params=_cp("parallel"),
        name="conv_swiglu_tm",
    )(u, hist, w, u, hist, w)


_NN = ((2,), (1,))
_NT = ((2,), (2,))
_TN = ((1,), (1,))


def _bdot(a, b, dims):
    return lax.dot_general(a.astype(BF16), b.astype(BF16), (dims, ((0,), (0,))), preferred_element_type=F32)


def _split_bf16(x, terms):
    parts, r = [], x
    for _ in range(terms):
        p = r.astype(BF16)
        parts.append(p)
        r = r - p.astype(F32)
    return parts


def _l2n(x):
    return x * lax.rsqrt(jnp.sum(x * x, axis=-1, keepdims=True) + EPS)


def _dn_chunk_terms(q, k, v, beta, g):
    n, c, _ = q.shape
    ii = lax.broadcasted_iota(jnp.int32, (c, c), 0)
    jj = lax.broadcasted_iota(jnp.int32, (c, c), 1)
    incl = jj <= ii
    gm = jnp.broadcast_to(g, (n, c, c))
    ltri = jnp.broadcast_to(incl.astype(BF16)[None], (n, c, c))
    ones = jnp.ones((n, c, c), BF16)
    gc = sum(_bdot(ltri, p, _NN) for p in _split_bf16(gm, 3))
    gr = sum(_bdot(ones, p, _NN) for p in _split_bf16(jnp.where(ii <= jj, gm, 0.0), 3))
    decay = jnp.where(incl, jnp.exp(gc - gr), 0.0)
    gcol = gc[:, :, 0:1]
    eg = jnp.exp(gcol)
    gend = gc[:, c - 1:c, 0:1]
    k_hi, k_lo = _split_bf16(k, 2)
    kk = _bdot(k_hi, k_hi, _NT) + (_bdot(k_hi, k_lo, _NT) + _bdot(k_lo, k_hi, _NT))
    lo = jnp.where(jj < ii, beta * kk * decay, 0.0)
    inv = jnp.where(ii == jj, 1.0, 0.0) - lo
    pw = lo
    for _ in range(int(math.log2(c)) - 1):
        pw = _bdot(pw, pw, _NN)
        inv = inv + _bdot(inv, pw, _NN)
    u_v = _bdot(inv, beta * v, _NN)
    w_k = _bdot(inv, (beta * eg) * k, _NN)
    qk = _bdot(q, k, _NT) * decay
    return u_v, w_k, qk, q * eg, k * jnp.exp(gend - gcol), jnp.exp(gend)


def _dn_out(o, gn, z):
    return (_rms(o, gn) * _silu(z)).astype(BF16)


def _conv_silu_seq(x_ref, hist_ref, w_ref, xs_ref, width):
    seq = x_ref.shape[0]
    xs_ref[0:SUBLANES, :] = hist_ref[0]
    xs_ref[SUBLANES:, :] = x_ref[...]
    y = x_ref[...] * w_ref[width - 1:width, :]
    for i in range(width - 1):
        y = y + xs_ref[pl.ds(SUBLANES - (width - 1) + i, seq), :] * w_ref[i:i + 1, :]
    return _silu(y)


def _dn_prompt_body(alog_ref, dtb_ref, xq_ref, xk_ref, xv_ref, hq_ref, hk_ref, hv_ref, wq_ref, wk_ref, wv_ref,
                    ba_ref, z_ref, gn_ref, s0_ref, o_ref, s_ref, kw_s, ku_s, ge_s, sp_s, xs_s, *,
                    chunk, hb, width, b_lane, a_lane):
    t = xq_ref.shape[0]
    n = t // chunk
    d = DN_HEAD
    lane = lax.broadcasted_iota(jnp.int32, (1, LANES), 1)
    ba = ba_ref[...]
    r3 = lambda x: x.reshape(n, chunk, x.shape[-1])
    yq = _conv_silu_seq(xq_ref, hq_ref, wq_ref, xs_s, width)
    yk = _conv_silu_seq(xk_ref, hk_ref, wk_ref, xs_s, width)
    yv = _conv_silu_seq(xv_ref, hv_ref, wv_ref, xs_s, width)
    q_w, q_u = [], []
    for i in range(hb):
        h = pl.program_id(1) * hb + i
        cols = slice(i * d, (i + 1) * d)
        q = _l2n(yq[:, cols]) * (d ** -0.5)
        k = _l2n(yk[:, cols])
        b_raw = jnp.sum(jnp.where(lane == b_lane + h, ba, 0.0), axis=1, keepdims=True)
        a_raw = jnp.sum(jnp.where(lane == a_lane + h, ba, 0.0), axis=1, keepdims=True)
        beta = _sigmoid(b_raw)
        g = -jnp.exp(alog_ref[h]) * _softplus(a_raw + dtb_ref[h])
        u_v, w_k, qk, q_dec, k_dec, g_end = _dn_chunk_terms(r3(q), r3(k), r3(yv[:, cols]), r3(beta), r3(g))
        kw_s[i] = _bdot(k_dec, w_k, _TN).astype(BF16)
        ku_s[i] = _bdot(k_dec, u_v, _TN)
        ge_s[i] = jnp.broadcast_to(g_end, (n, 1, d))
        q_w.append(q_dec - _bdot(qk, w_k, _NN))
        q_u.append(_bdot(qk, u_v, _NN))

    def step(c, ss):
        out = []
        for i, s in enumerate(ss):
            sb = s.astype(BF16)
            sp_s[i, c] = sb
            out.append(ge_s[i, c] * s - _dot(kw_s[i, c], sb) + ku_s[i, c])
        return tuple(out)

    finals = lax.fori_loop(0, n, step, tuple(s0_ref[0, i] for i in range(hb)))
    for i in range(hb):
        s_ref[0, i] = finals[i]
        o = _bdot(q_w[i], sp_s[i], _NN) + q_u[i]
        cols = slice(i * d, (i + 1) * d)
        o_ref[:, cols] = _dn_out(o.reshape(t, d), gn_ref[...], z_ref[:, cols])


def deltanet_prompt(x, hist, conv_w, small, z, a_log, dt_bias, out_norm, s0, bsz, seq, nh, hb, ba_block, b_lane,
                    a_lane):
    n = x.shape[0]
    width = conv_w.shape[0]
    chunk = min(DN_CHUNK, seq)
    d = DN_HEAD
    nc = seq // chunk
    assert nh % hb == 0
    h8 = _hist8(hist)
    smem = pl.BlockSpec(memory_space=pltpu.SMEM)
    col = lambda off: pl.BlockSpec((seq, hb * d), lambda b, h, off=off: (b, off // hb + h))
    hcol = lambda off: pl.BlockSpec((1, SUBLANES, hb * d), lambda b, h, off=off: (b, 0, off // hb + h))
    wcol = lambda off: pl.BlockSpec((width, hb * d), lambda b, h, off=off: (0, off // hb + h))
    offs = (0, nh, 2 * nh)
    return pl.pallas_call(
        functools.partial(_dn_prompt_body, chunk=chunk, hb=hb, width=width, b_lane=b_lane, a_lane=a_lane),
        grid=(bsz, nh // hb),
        in_specs=[smem, smem, *[col(o) for o in offs], *[hcol(o) for o in offs], *[wcol(o) for o in offs],
                  pl.BlockSpec((seq, LANES), lambda b, h: (b, ba_block)),
                  pl.BlockSpec((seq, hb * d), lambda b, h: (b, h)),
                  pl.BlockSpec((1, d), lambda b, h: (0, 0)),
                  pl.BlockSpec((1, hb, d, d), lambda b, h: (b, h, 0, 0))],
        out_specs=[pl.BlockSpec((seq, hb * d), lambda b, h: (b, h)),
                   pl.BlockSpec((1, hb, d, d), lambda b, h: (b, h, 0, 0))],
        out_shape=[jax.ShapeDtypeStruct((n, nh * d), BF16), jax.ShapeDtypeStruct((bsz, nh, d, d), F32)],
        scratch_shapes=[pltpu.VMEM((hb, nc, d, d), BF16), pltpu.VMEM((hb, nc, d, d), F32),
                        pltpu.VMEM((hb, nc, 1, d), F32), pltpu.VMEM((hb, nc, d, d), BF16),
                        pltpu.VMEM((seq + SUBLANES, hb * d), F32)],
        compiler_params=_cp("parallel", "parallel"),
        name="deltanet_prompt",
    )(a_log, dt_bias, x, x, x, h8, h8, h8, conv_w, conv_w, conv_w, small, z, out_norm.reshape(1, d), s0)


def _dn_sample_body(q_ref, k_ref, v_ref, b_ref, a_ref, z_ref, alog_ref, dtb_ref, gn_ref, s0_ref, *rest, t_real):
    o_ref, s_ref = rest[-2:]
    q = _l2n(q_ref[0]) * (DN_HEAD ** -0.5)
    k = _l2n(k_ref[0])
    v = v_ref[0]
    real = lax.broadcasted_iota(jnp.int32, (q.shape[1], 1), 0) < t_real
    beta = jnp.where(real, _sigmoid(b_ref[0][:, :, 0:1]), 0.0)
    g = jnp.where(real, -jnp.exp(alog_ref[...][:, :, 0:1]) * _softplus(a_ref[0][:, :, 0:1] + dtb_ref[...][:, :, 0:1]), 0.0)
    u_v, w_k, qk, q_dec, k_dec, g_end = _dn_chunk_terms(q, k, v, beta, g)
    s = s0_ref[0, 0]
    u = u_v - _bdot(w_k, s, _NN)
    o = _bdot(q_dec, s, _NN) + _bdot(qk, u, _NN)
    o_ref[0] = _dn_out(o, gn_ref[...], z_ref[0])
    s_ref[0, 0] = g_end * s + _bdot(k_dec, u, _TN)


def deltanet_sample(q, k, v, b_raw, a_raw, z, a_log, dt_bias, out_norm, states, new_states, layer, t_real):
    bsz, nh, tp, d = q.shape
    blk = pl.BlockSpec((1, nh, tp, d), lambda b: (b, 0, 0, 0))
    par = pl.BlockSpec((nh, 1, d), lambda b: (0, 0, 0))
    st = pl.BlockSpec((1, 1, nh, d, d), lambda b: (layer, b, 0, 0, 0))
    bc = lambda p: jnp.broadcast_to(p.astype(F32).reshape(nh, 1, 1), (nh, 1, d))
    in_specs = [blk, blk, blk, blk, blk, blk, par, par, pl.BlockSpec((1, d), lambda b: (0, 0)), st]
    args = [q, k, v, b_raw, a_raw, z, bc(a_log), bc(dt_bias), out_norm.reshape(1, d), states]
    aliases = {}
    if new_states is not None:
        aliases = {len(args): 1}
        in_specs.append(pl.BlockSpec(memory_space=pl.ANY))
        args.append(new_states)
    return pl.pallas_call(
        functools.partial(_dn_sample_body, t_real=t_real),
        grid=(bsz,),
        in_specs=in_specs,
        out_specs=[blk, st],
        out_shape=[jax.ShapeDtypeStruct((bsz, nh, tp, d), BF16), jax.ShapeDtypeStruct(states.shape, F32)],
        input_output_aliases=aliases,
        compiler_params=_cp("parallel"),
        name="deltanet_sample",
    )(*args)


def _merge_body(oa_ref, od_ref, om_ref, wa_ref, wd_ref, wm_ref, g0_ref, g1_ref, g2_ref, o_ref):
    acc = _sigmoid(g0_ref[...]) * _dot(oa_ref[...], wa_ref[...])
    acc = acc + _sigmoid(g1_ref[...]) * _dot(od_ref[...], wd_ref[...])
    acc = acc + _sigmoid(g2_ref[...]) * _dot(om_ref[...], wm_ref[...])
    o_ref[...] = acc.astype(o_ref.dtype)


def merge_branches(o_mla, o_dn, o_mem, w_mla, w_dn, w_mem, gate_raw, tm, tn, layer):
    n, d = o_mla.shape[0], w_mla.shape[1]
    nj = d // tn
    a_spec = lambda width: pl.BlockSpec((tm, width), lambda i, j: (i, 0))
    w_spec = lambda width: pl.BlockSpec((width, tn), lambda i, j: (layer, j))
    g_spec = lambda br: pl.BlockSpec((tm, tn), lambda i, j, br=br: (i, br * nj + j))
    return pl.pallas_call(
        _merge_body,
        grid=(n // tm, nj),
        in_specs=[a_spec(o_mla.shape[1]), a_spec(o_dn.shape[1]), a_spec(o_mem.shape[1]),
                  w_spec(o_mla.shape[1]), w_spec(o_dn.shape[1]), w_spec(o_mem.shape[1]),
                  g_spec(0), g_spec(1), g_spec(2)],
        out_specs=pl.BlockSpec((tm, tn), lambda i, j: (i, j)),
        out_shape=jax.ShapeDtypeStruct((n, d), BF16),
        compiler_params=_cp("parallel", "parallel"),
        name="merge_branches",
    )(o_mla, o_dn, o_mem, w_mla, w_dn, w_mem, gate_raw, gate_raw, gate_raw)


def _rope_tables(pos):
    half = QK_ROPE // 2
    inv_freq = ROPE_BASE ** (-jnp.arange(half, dtype=F32) / half)
    ang = pos.astype(F32)[:, None] * inv_freq[None, :]
    cos, sin = jnp.cos(ang), jnp.sin(ang)
    return jnp.concatenate([cos, cos], axis=1), jnp.concatenate([-sin, sin], axis=1)


PREF = dict(tm=1024, tn=1024, tn_small=512, tn_up=512, tn_down=512, norm_rows=256, mem_rows=512, flash_q=512,
            conv_cols=1024, ffn_rows=2048, ffn_cols=256, merge_m=512, merge_n=512, pages=32, pages_sub=32, dn_heads=2)


def _tile(n, pref, unit=1):
    if n <= pref:
        return n
    for t in range(pref - pref % unit, 0, -unit):
        if n % t == 0:
            return t
    return n


class _Weights:
    def __init__(self, w_in, w_uq, w_uk, w_uv, w_mem_kv, w_br_mla, w_br_dn, w_br_mem, w_out, w_up, w_down, dims):
        q_lora, kv_lora, dn_ch, nh_dn, dn_v, mem_d, d_model = dims
        depth = w_in.shape[0]
        o = 0
        cuts = {}
        for name, width in (("cq", q_lora), ("ckv", kv_lora), ("kpe", QK_ROPE), ("qkv", dn_ch), ("b", nh_dn),
                            ("a", nh_dn), ("z", dn_v), ("mq", mem_d), ("gate", 3 * d_model)):
            cuts[name] = (o, o + width)
            o += width
        rows = lambda w: w.astype(BF16).reshape(depth * w.shape[1], w.shape[2])
        c = lambda a, b: w_in[:, :, a:b]
        self.w_small = rows(jnp.concatenate([c(cuts["cq"][0], cuts["kpe"][1]), c(cuts["b"][0], cuts["a"][1])], axis=2))
        self.w_qkv = rows(c(*cuts["qkv"]))
        self.w_z = rows(c(*cuts["z"]))
        self.w_mq = rows(c(*cuts["mq"]))
        self.w_gate = rows(c(*cuts["gate"]))
        nh = w_uk.shape[1]
        heads = lambda w: w.astype(BF16).reshape((depth * nh,) + w.shape[2:])
        self.w_uq = heads(jnp.transpose(w_uq.reshape(depth, q_lora, nh, -1), (0, 2, 1, 3)))
        self.w_uk = heads(w_uk)
        self.w_uv = heads(w_uv)
        self.w_mem_kv = rows(w_mem_kv)
        self.w_br_mla = rows(w_br_mla)
        self.w_br_dn = rows(w_br_dn)
        self.w_br_mem = rows(w_br_mem)
        self.w_out = rows(w_out)
        self.w_up = rows(w_up)
        self.w_down = rows(w_down)


def _token_stage(x, wts, norm_mix, tm, l):
    h = rmsnorm(x, norm_mix, BF16, _tile(x.shape[0], PREF["norm_rows"], SUBLANES))
    mm = lambda w, tn: matmul(h, w, F32, tm, _tile(w.shape[1], tn, LANES), layer=l)
    tn, tns = PREF["tn"], PREF["tn_small"]
    return mm(wts.w_small, tns), mm(wts.w_qkv, tn), mm(wts.w_z, tn), mm(wts.w_mq, tns), mm(wts.w_gate, tn)


def _ffn_tail(x, merged, wts, norm_ffn, tm, l):
    x = matmul(merged, wts.w_out, F32, tm, _tile(wts.w_out.shape[1], PREF["tn"], LANES), res=x, layer=l)
    h = rmsnorm(x, norm_ffn, BF16, _tile(x.shape[0], PREF["norm_rows"], SUBLANES))
    u = matmul(h, wts.w_up, F32, tm, _tile(wts.w_up.shape[1], PREF["tn_up"], LANES), layer=l)
    return x, u


def _down(act, wts, x, tm, l):
    d_ff, d_model = act.shape[1], wts.w_down.shape[1]
    return matmul(act, wts.w_down, F32, tm, _tile(d_model, PREF["tn_down"], LANES), tk=_tile(d_ff, d_ff // 2, LANES),
                  res=x, layer=l)


def kernel(x_prompt, x_sample, mem_prompt, cache_ckv, cache_kpe, page_table, cache_mem_k, cache_mem_v, state_dn,
           state_dn_conv, state_ffn_conv, norm_mix, w_in, q_norm, kv_norm, kpe_norm, qh_norm, w_uq, w_uk, w_uv,
           dn_conv_w, dn_a_log, dn_dt_bias, dn_out_norm, mem_norm, w_mem_kv, mk_norm, mq_norm, w_br_mla, w_br_dn,
           w_br_mem, w_out, norm_ffn, w_up, ffn_conv_w, w_down):
    bp, tp, d_model = x_prompt.shape
    bs, ts, _ = x_sample.shape
    depth = w_in.shape[0]
    nh = w_uk.shape[1]
    q_lora = w_uq.shape[1]
    kv_lora = w_uk.shape[3]
    nh_dn = dn_a_log.shape[1]
    dn_v = w_br_dn.shape[1]
    dn_ch = dn_conv_w.shape[2]
    dn_qk = (dn_ch - dn_v) // 2
    mem_tok = mem_prompt.shape[1]
    mem_d = w_br_mem.shape[1]
    d_ff = w_down.shape[1]
    past_len = page_table.shape[1] * cache_ckv.shape[2]
    scale = (QK_NOPE + QK_ROPE) ** -0.5
    dims = (q_lora, kv_lora, dn_ch, nh_dn, dn_v, mem_d, d_model)
    ba_off = q_lora + kv_lora + QK_ROPE
    ba_block, b_lane = divmod(ba_off, LANES)
    a_lane = b_lane + nh_dn
    assert a_lane + nh_dn <= LANES and dn_qk == nh_dn * DN_HEAD

    cos_p, sin_p = _rope_tables(jnp.arange(tp, dtype=jnp.int32))
    cos_s, sin_s = _rope_tables(past_len + jnp.arange(ts, dtype=jnp.int32))
    cos_s, sin_s = jnp.tile(cos_s, (bs, 1)), jnp.tile(sin_s, (bs, 1))

    n_p, n_s = bp * tp, bs * ts
    tm_p = _tile(n_p, PREF["tm"], SUBLANES)
    tm_s = _tile(n_s, PREF["tm"], SUBLANES)
    mem_tm = _tile(bp * mem_tok, PREF["mem_rows"], SUBLANES)
    tp_pad = SUBLANES * -(-ts // SUBLANES)
    xp = x_prompt.reshape(n_p, d_model)
    xs = x_sample.reshape(n_s, d_model)
    mem_rows = mem_prompt.reshape(bp * mem_tok, d_model)
    pages_per_step = _tile(page_table.shape[1], PREF["pages"])
    cache_kpe_t = jnp.swapaxes(cache_kpe, 2, 3)
    zeros_state = jnp.zeros((bp, nh_dn, DN_HEAD, DN_HEAD), F32)
    zeros_dn_hist = jnp.zeros((bp, dn_conv_w.shape[1] - 1, dn_ch), F32)
    zeros_ffn_hist = jnp.zeros((bp, ffn_conv_w.shape[1] - 1, 2 * d_ff), F32)

    outs_p = {k: [] for k in ("ckv", "kpe", "dns", "dnh", "ffh", "mk", "mv")}
    outs_s = {k: [] for k in ("ckv", "kpe", "dnh", "ffh")}
    dns_s = None

    wts = _Weights(w_in, w_uq, w_uk, w_uv, w_mem_kv, w_br_mla, w_br_dn, w_br_mem, w_out, w_up, w_down, dims)
    for l in range(depth):
        dn_w = dn_conv_w[l]
        ffn_w = ffn_conv_w[l]
        dn_wd = dn_w.shape[0]
        ffn_wd = ffn_w.shape[0]

        mem_h = rmsnorm(mem_rows, mem_norm[l], BF16, _tile(bp * mem_tok, PREF["norm_rows"], SUBLANES))
        mem_kv = matmul(mem_h, wts.w_mem_kv, F32, mem_tm, _tile(2 * mem_d, PREF["tn_small"], LANES), layer=l)
        mk_p = headnorm(mem_kv[:, :mem_d], mk_norm[l], MEM_HD, mem_tm)
        mv_p = mem_kv[:, mem_d:]

        small, qkv, z, mq, gate_raw = _token_stage(xp, wts, norm_mix[l], tm_p, l)
        ql, qr, ckv, ckvb, kpe, kpeb = mla_prep(small, q_norm[l], kv_norm[l], kpe_norm[l], qh_norm[l], wts.w_uq,
                                                wts.w_uk, cos_p, sin_p, _tile(tp, PREF["tm"], SUBLANES), scale, nh, l)
        o_mla = flash_prompt(ql, qr, ckvb, kpeb, wts.w_uv, bp, tp, _tile(tp, PREF["flash_q"], SUBLANES), l)
        o_dn, dns_p = deltanet_prompt(qkv, zeros_dn_hist, dn_w, small, z, dn_a_log[l], dn_dt_bias[l], dn_out_norm[l],
                                      zeros_state,
                                      bp, tp, nh_dn, _tile(nh_dn, PREF["dn_heads"]), ba_block, b_lane, a_lane)
        o_mem = mem_attend(mq.reshape(bp, tp, mem_d), mk_p.reshape(bp, mem_tok, mem_d),
                           mv_p.reshape(bp, mem_tok, mem_d), mq_norm[l], _tile(tp, PREF["flash_q"], SUBLANES)).reshape(n_p, mem_d)
        merged = merge_branches(o_mla, o_dn, o_mem, wts.w_br_mla, wts.w_br_dn, wts.w_br_mem, gate_raw,
                                _tile(n_p, PREF["merge_m"], SUBLANES), _tile(d_model, PREF["merge_n"], LANES), l)
        xp, u = _ffn_tail(xp, merged, wts, norm_ffn[l], tm_p, l)
        act = conv_swiglu_rows(u, zeros_ffn_hist, ffn_w, tp, _tile(tp, PREF["ffn_rows"], SUBLANES),
                               _tile(d_ff, PREF["ffn_cols"], LANES))
        xp = _down(act, wts, xp, tm_p, l)

        outs_p["ckv"].append(ckv.reshape(bp, tp, kv_lora))
        outs_p["kpe"].append(kpe.reshape(bp, tp, QK_ROPE))
        outs_p["dns"].append(dns_p)
        outs_p["dnh"].append(qkv.reshape(bp, tp, dn_ch)[:, tp - (dn_wd - 1):])
        outs_p["ffh"].append(u.reshape(bp, tp, 2 * d_ff)[:, tp - (ffn_wd - 1):])
        outs_p["mk"].append(mk_p.reshape(bp, mem_tok, mem_d // MEM_HD, MEM_HD))
        outs_p["mv"].append(mv_p.reshape(bp, mem_tok, mem_d // MEM_HD, MEM_HD))

        small, qkv, z, mq, gate_raw = _token_stage(xs, wts, norm_mix[l], tm_s, l)
        ql, qr, ckv, ckvb, kpe, kpeb = mla_prep(small, q_norm[l], kv_norm[l], kpe_norm[l], qh_norm[l], wts.w_uq,
                                                wts.w_uk, cos_s, sin_s, tm_s, scale, nh, l)
        to_bht = lambda a: jnp.transpose(a.reshape(nh, bs, ts, -1), (1, 0, 2, 3)).reshape(bs, nh * ts, -1)
        o_lat = paged_sample(to_bht(ql), to_bht(qr), ckv.reshape(bs, ts, kv_lora), kpe.reshape(bs, ts, QK_ROPE),
                             cache_ckv, cache_kpe_t, page_table, l, pages_per_step,
                             _tile(pages_per_step, PREF["pages_sub"]))
        o_lat = jnp.transpose(o_lat.reshape(bs, nh, ts, kv_lora), (1, 0, 2, 3)).reshape(nh, n_s, kv_lora)
        o_mla = uv_project(o_lat, wts.w_uv, tm_s, l)

        hist = state_dn_conv[l]
        qkv_tm = jnp.transpose(qkv.reshape(bs, ts, dn_ch), (1, 0, 2))
        y_tm = conv_silu_tm(qkv_tm, jnp.transpose(hist, (1, 0, 2)), dn_w, _tile(dn_ch, PREF["conv_cols"], LANES))
        pad_t = lambda a: jnp.pad(a, ((0, 0), (0, 0), (0, tp_pad - ts), (0, 0)))
        y5 = pad_t(jnp.transpose(y_tm.reshape(ts, bs, 3 * nh_dn, DN_HEAD), (1, 2, 0, 3)))
        z4 = pad_t(jnp.transpose(z.reshape(bs, ts, nh_dn, DN_HEAD), (0, 2, 1, 3)))
        ba = jnp.transpose(small[:, ba_off:ba_off + 2 * nh_dn].reshape(bs, ts, 2 * nh_dn), (0, 2, 1))[..., None]
        ba = pad_t(jnp.broadcast_to(ba, (bs, 2 * nh_dn, ts, DN_HEAD)))
        o_dn, dns_s = deltanet_sample(y5[:, :nh_dn], y5[:, nh_dn:2 * nh_dn], y5[:, 2 * nh_dn:], ba[:, :nh_dn],
                                      ba[:, nh_dn:], z4, dn_a_log[l], dn_dt_bias[l], dn_out_norm[l], state_dn, dns_s,
                                      l, ts)
        o_dn = jnp.transpose(o_dn[:, :, :ts], (0, 2, 1, 3)).reshape(n_s, dn_v)
        dnh_s = jnp.concatenate([hist, qkv.reshape(bs, ts, dn_ch)], axis=1)[:, ts:]

        mq_pad = jnp.pad(mq.reshape(bs, ts, mem_d), ((0, 0), (0, tp_pad - ts), (0, 0)))
        o_mem = mem_attend(mq_pad, cache_mem_k.reshape(depth * bs, mem_tok, mem_d),
                           cache_mem_v.reshape(depth * bs, mem_tok, mem_d), mq_norm[l], tp_pad, kv_first=l * bs)
        o_mem = o_mem[:, :ts].reshape(n_s, mem_d)
        merged = merge_branches(o_mla, o_dn, o_mem, wts.w_br_mla, wts.w_br_dn, wts.w_br_mem, gate_raw,
                                _tile(n_s, PREF["merge_m"], SUBLANES), _tile(d_model, PREF["merge_n"], LANES), l)
        xs, u = _ffn_tail(xs, merged, wts, norm_ffn[l], tm_s, l)
        fh = state_ffn_conv[l]
        u3 = u.reshape(bs, ts, 2 * d_ff)
        act = conv_swiglu_tm(jnp.transpose(u3, (1, 0, 2)), jnp.transpose(fh, (1, 0, 2)), ffn_w,
                             _tile(d_ff, PREF["ffn_cols"], LANES))
        act = jnp.transpose(act, (1, 0, 2)).reshape(n_s, d_ff)
        xs = _down(act, wts, xs, tm_s, l)

        outs_s["ckv"].append(ckv.reshape(bs, ts, kv_lora))
        outs_s["kpe"].append(kpe.reshape(bs, ts, QK_ROPE))
        outs_s["dnh"].append(dnh_s)
        outs_s["ffh"].append(jnp.concatenate([fh, u3], axis=1)[:, ts:])

    st = lambda d, k: jnp.stack(d[k])
    return (xp.reshape(bp, tp, d_model), xs.reshape(bs, ts, d_model),
            st(outs_p, "ckv"), st(outs_p, "kpe"), st(outs_p, "dns"), st(outs_p, "dnh"), st(outs_p, "ffh"),
            st(outs_p, "mk"), st(outs_p, "mv"),
            st(outs_s, "ckv"), st(outs_s, "kpe"), dns_s, st(outs_s, "dnh"), st(outs_s, "ffh"))
```
